```python
import math
import jax, jax.numpy as jnp
from jax import lax
import numpy as np

D_MODEL = 1024
BATCH = 4
SEQ = 4096
DEPTH = 4

N_MIXERS = 2
N_RET_LAYERS = (DEPTH + N_MIXERS - 1) // N_MIXERS
N_SB_LAYERS = DEPTH // N_MIXERS
PLE_DIM = 256
D_FF = 2816
FFN_RES_WEIGHT = 0.5
RET_HEADS = 4
RET_DK = D_MODEL // RET_HEADS
RET_QK = RET_HEADS * RET_DK
RET_DV = 2 * RET_DK
RET_V = RET_HEADS * RET_DV
RET_IN = 2 * RET_QK + 2 * RET_V
RET_CHUNK = 128
ROPE_BASE = 10000.0
GN_EPS = 1e-5
SB_HEADS = 8
SB_DH = D_MODEL // SB_HEADS
SB_WIDTH = SB_HEADS * SB_DH
SB_IN = 3 * SB_WIDTH
SB_BLOCK = 128
N_NORMS = 8
RMS_EPS = 1e-6

kernel_name = "hybrid_retention_stickbreaking_macaron_trunk"


def rmsnorm(x, g):
    xf = x.astype(jnp.float32)
    y = xf * lax.rsqrt(jnp.mean(xf * xf, axis=-1, keepdims=True) + RMS_EPS)
    return (y * g.astype(jnp.float32)).astype(x.dtype)


def swiglu(x, w_gate, w_up, w_down):
    return (jax.nn.silu(x @ w_gate) * (x @ w_up)) @ w_down


def rope(t, positions):
    half = t.shape[-1] // 2
    inv = ROPE_BASE ** (-jnp.arange(half, dtype=jnp.float32) / half)
    ang = positions.astype(jnp.float32)[..., None] * inv
    cos = jnp.cos(ang)[:, :, None, :]
    sin = jnp.sin(ang)[:, :, None, :]
    t1, t2 = t[..., :half], t[..., half:]
    return jnp.concatenate([t1 * cos - t2 * sin, t1 * sin + t2 * cos], axis=-1)


def retention(h, positions, w_in, gn_gain, w_out):
    B, S, _ = h.shape
    proj = h @ w_in
    q, k, v, g = jnp.split(proj, [RET_QK, 2 * RET_QK, 2 * RET_QK + RET_V], axis=-1)
    q = rope(q.astype(jnp.float32).reshape(B, S, RET_HEADS, RET_DK), positions)
    k = rope(k.astype(jnp.float32).reshape(B, S, RET_HEADS, RET_DK), positions) * (RET_DK ** -0.5)
    v = v.astype(jnp.float32).reshape(B, S, RET_HEADS, RET_DV)

    nc = S // RET_CHUNK
    def to_chunks(t):
        return t.reshape(B, nc, RET_CHUNK, RET_HEADS, -1).transpose(1, 0, 3, 2, 4)
    qc, kc, vc = to_chunks(q), to_chunks(k), to_chunks(v)

    log_gamma = jnp.log1p(-jnp.exp2(-5.0 - jnp.arange(RET_HEADS, dtype=jnp.float32)))
    idx = jnp.arange(RET_CHUNK, dtype=jnp.float32)
    rel = idx[:, None] - idx[None, :]
    inner_decay = jnp.where(rel[None] >= 0,
                            jnp.exp(jnp.maximum(rel, 0.0)[None] * log_gamma[:, None, None]),
                            0.0)
    xi = jnp.exp((idx + 1.0)[None, :] * log_gamma[:, None])
    zeta = jnp.exp((RET_CHUNK - 1.0 - idx)[None, :] * log_gamma[:, None])
    chunk_decay = jnp.exp(RET_CHUNK * log_gamma)

    def step(state, inp):
        qi, ki, vi = inp
        scores = jnp.einsum('bhid,bhjd->bhij', qi, ki) * inner_decay[None]
        o_inner = jnp.einsum('bhij,bhjv->bhiv', scores, vi)
        o_cross = jnp.einsum('bhid,bhdv->bhiv', qi, state) * xi[None, :, :, None]
        new_state = state * chunk_decay[None, :, None, None] + jnp.einsum(
            'bhjd,bhjv->bhdv', ki * zeta[None, :, :, None], vi)
        return new_state, o_inner + o_cross

    state0 = jnp.zeros((B, RET_HEADS, RET_DK, RET_DV), jnp.float32)
    _, o = lax.scan(step, state0, (qc, kc, vc))
    o = o.transpose(1, 0, 3, 2, 4).reshape(B, S, RET_HEADS, RET_DV)
    mu = jnp.mean(o, axis=-1, keepdims=True)
    var = jnp.mean(jnp.square(o - mu), axis=-1, keepdims=True)
    o = ((o - mu) * lax.rsqrt(var + GN_EPS)).reshape(B, S, RET_V) * gn_gain.astype(jnp.float32)
    o = jax.nn.silu(g.astype(jnp.float32)) * o
    return o.astype(h.dtype) @ w_out


def stick_breaking(h, w_in, w_out):
    B, S, _ = h.shape
    proj = h @ w_in
    q, k, v = jnp.split(proj, [SB_WIDTH, 2 * SB_WIDTH], axis=-1)
    def heads(t):
        return t.astype(jnp.float32).reshape(B, S, SB_HEADS, SB_DH).transpose(0, 2, 1, 3)
    q, k, v = heads(q), heads(k), heads(v)
    scale = SB_DH ** -0.5
    nb = S // SB_BLOCK
    qb = q.reshape(B, SB_HEADS, nb, SB_BLOCK, SB_DH).transpose(2, 0, 1, 3, 4)
    kpos = jnp.arange(S)

    def block(args):
        q_blk, b = args
        qpos = b * SB_BLOCK + jnp.arange(SB_BLOCK)
        z = jnp.einsum('bhqd,bhkd->bhqk', q_blk, k) * scale
        mask = (kpos[None, :] < qpos[:, None])[None, None]
        log_one_minus = jnp.where(mask, jax.nn.log_sigmoid(-z), 0.0)
        rem = lax.cumsum(log_one_minus, axis=3, reverse=True) - log_one_minus
        a = jnp.where(mask, jnp.exp(jax.nn.log_sigmoid(z) + rem), 0.0)
        return jnp.einsum('bhqk,bhkd->bhqd', a, v)

    o = lax.map(block, (qb, jnp.arange(nb)))
    o = o.transpose(1, 0, 3, 2, 4).reshape(B, S, SB_WIDTH)
    return o.astype(h.dtype) @ w_out


def setup_inputs(seed: int = 0) -> dict:
    key = jax.random.key(seed)
    ks = jax.random.split(key, 16)
    f32 = jnp.float32
    def w(k, shape, fan_in):
        return jax.random.normal(k, shape, f32) * (fan_in ** -0.5)
    x = jax.random.normal(ks[0], (BATCH, SEQ, D_MODEL), f32)
    p = jax.random.normal(ks[1], (DEPTH, BATCH, SEQ, PLE_DIM), f32)
    positions = jnp.broadcast_to(jnp.arange(SEQ, dtype=jnp.int32), (BATCH, SEQ))
    norm_gains = 1.0 + 0.05 * jax.random.normal(ks[2], (DEPTH, N_NORMS, D_MODEL), f32)
    ffn_w_gate = w(ks[3], (DEPTH, 2, D_MODEL, D_FF), D_MODEL)
    ffn_w_up = w(ks[4], (DEPTH, 2, D_MODEL, D_FF), D_MODEL)
    ffn_w_down = w(ks[5], (DEPTH, 2, D_FF, D_MODEL), D_FF)
    ret_w_in = w(ks[6], (N_RET_LAYERS, D_MODEL, RET_IN), D_MODEL)
    ret_gn_gain = 1.0 + 0.05 * jax.random.normal(ks[7], (N_RET_LAYERS, RET_V), f32)
    ret_w_out = w(ks[8], (N_RET_LAYERS, RET_V, D_MODEL), RET_V)
    sb_w_in = w(ks[9], (N_SB_LAYERS, D_MODEL, SB_IN), D_MODEL)
    sb_w_out = w(ks[10], (N_SB_LAYERS, SB_WIDTH, D_MODEL), SB_WIDTH)
    ple_w_gate = w(ks[11], (DEPTH, D_MODEL, D_MODEL), D_MODEL)
    ple_w_proj = w(ks[12], (DEPTH, PLE_DIM, D_MODEL), PLE_DIM)
    return {"x": x, "p": p, "positions": positions, "norm_gains": norm_gains,
            "ffn_w_gate": ffn_w_gate, "ffn_w_up": ffn_w_up, "ffn_w_down": ffn_w_down,
            "ret_w_in": ret_w_in, "ret_gn_gain": ret_gn_gain, "ret_w_out": ret_w_out,
            "sb_w_in": sb_w_in, "sb_w_out": sb_w_out,
            "ple_w_gate": ple_w_gate, "ple_w_proj": ple_w_proj}


def reference(x, p, positions, norm_gains, ffn_w_gate, ffn_w_up, ffn_w_down,
              ret_w_in, ret_gn_gain, ret_w_out, sb_w_in, sb_w_out,
              ple_w_gate, ple_w_proj):
    h = x
    for i in range(DEPTH):
        g = norm_gains[i]
        f = swiglu(rmsnorm(h, g[0]), ffn_w_gate[i, 0], ffn_w_up[i, 0], ffn_w_down[i, 0])
        h = h + FFN_RES_WEIGHT * rmsnorm(f, g[1])
        m_in = rmsnorm(h, g[2])
        j = i // N_MIXERS
        if i % N_MIXERS == 0:
            m = retention(m_in, positions, ret_w_in[j], ret_gn_gain[j], ret_w_out[j])
        else:
            m = stick_breaking(m_in, sb_w_in[j], sb_w_out[j])
        h = h + rmsnorm(m, g[3])
        f = swiglu(rmsnorm(h, g[4]), ffn_w_gate[i, 1], ffn_w_up[i, 1], ffn_w_down[i, 1])
        h = h + FFN_RES_WEIGHT * rmsnorm(f, g[5])
        gate = jax.nn.sigmoid(rmsnorm(h, g[6]) @ ple_w_gate[i])
        e = p[i] @ ple_w_proj[i]
        h = h + rmsnorm(gate * e, g[7])
    return h
```

```python
import functools

import jax
import jax.numpy as jnp
from jax import lax
from jax.experimental import pallas as pl
from jax.experimental.pallas import tpu as pltpu

F32 = jnp.float32
BF16 = jnp.bfloat16

D_MODEL = 1024
DEPTH = 4
N_MIXERS = 2
PLE_DIM = 256
D_FF = 2816
FFN_RES_WEIGHT = 0.5
RET_HEADS = 4
RET_DK = D_MODEL // RET_HEADS
RET_QK = RET_HEADS * RET_DK
RET_DV = 2 * RET_DK
RET_V = RET_HEADS * RET_DV
RET_IN = 2 * RET_QK + 2 * RET_V
RET_CHUNK = 128
ROPE_BASE = 10000.0
ROPE_HALF = RET_DK // 2
GN_EPS = 1e-5
SB_HEADS = 8
SB_DH = D_MODEL // SB_HEADS
SB_WIDTH = SB_HEADS * SB_DH
SB_IN = 3 * SB_WIDTH
SB_BLOCK = 128
N_NORMS = 8
RMS_EPS = 1e-6

VMEM_LIMIT_BYTES = 56 * 1024 * 1024

FFN_TM = 1024
FFN_TF = 256
TOK_TM = 512
RET_STEP = 512

SB_SKIP_LOG = -105.0


def _params(sem):
    return pltpu.CompilerParams(dimension_semantics=sem,
                                vmem_limit_bytes=VMEM_LIMIT_BYTES)


def _resident(shape, index_map):
    return pl.BlockSpec(shape, index_map, pipeline_mode=pl.Buffered(1))


def _rms(x, g):
    ms = jnp.mean(x * x, axis=-1, keepdims=True)
    return x * lax.rsqrt(ms + RMS_EPS) * g


def _dot(a, b):
    return jnp.dot(a, b, preferred_element_type=F32)


def _silu(x):
    return x * jax.nn.sigmoid(x)


def _ffn_kernel(h_ref, gpre_ref, gpost_ref, wg_ref, wu_ref, wd_ref, o_ref,
                xn_ref, acc_ref):
    f = pl.program_id(1)

    @pl.when(f == 0)
    def _():
        xn_ref[...] = _rms(h_ref[...], gpre_ref[...]).astype(BF16)

    xn = xn_ref[...]
    hid = (_silu(_dot(xn, wg_ref[...])) * _dot(xn, wu_ref[...])).astype(BF16)
    part = _dot(hid, wd_ref[...])

    @pl.when(f == 0)
    def _():
        acc_ref[...] = part

    @pl.when(f > 0)
    def _():
        acc_ref[...] += part

    @pl.when(f == pl.num_programs(1) - 1)
    def _():
        o_ref[...] = h_ref[...] + FFN_RES_WEIGHT * _rms(acc_ref[...], gpost_ref[...])


def _ffn(h, gains, layer, which, wg, wu, wd):
    n = h.shape[0]
    g_pre = layer * N_NORMS + (0 if which == 0 else 4)
    g_post = g_pre + 1
    return pl.pallas_call(
        _ffn_kernel,
        grid=(n // FFN_TM, D_FF // FFN_TF),
        in_specs=[
            pl.BlockSpec((FFN_TM, D_MODEL), lambda i, f: (i, 0)),
            pl.BlockSpec((None, 1, D_MODEL), lambda i, f: (g_pre, 0, 0)),
            pl.BlockSpec((None, 1, D_MODEL), lambda i, f: (g_post, 0, 0)),
            pl.BlockSpec((None, None, D_MODEL, FFN_TF), lambda i, f: (layer, which, 0, f)),
            pl.BlockSpec((None, None, D_MODEL, FFN_TF), lambda i, f: (layer, which, 0, f)),
            pl.BlockSpec((None, None, FFN_TF, D_MODEL), lambda i, f: (layer, which, f, 0)),
        ],
        out_specs=pl.BlockSpec((FFN_TM, D_MODEL), lambda i, f: (i, 0)),
        out_shape=jax.ShapeDtypeStruct(h.shape, F32),
        scratch_shapes=[pltpu.VMEM((FFN_TM, D_MODEL), BF16),
                        pltpu.VMEM((FFN_TM, D_MODEL), F32)],
        compiler_params=_params(("parallel", "arbitrary")),
        name="ffn",
    )(h, gains, gains, wg, wu, wd)


def _ple_kernel(h_ref, p_ref, g6_ref, g7_ref, wgate_ref, wproj_ref, o_ref):
    h = h_ref[...]
    xn = _rms(h, g6_ref[...]).astype(BF16)
    gate = jax.nn.sigmoid(_dot(xn, wgate_ref[...]))
    e = _dot(p_ref[...].astype(BF16), wproj_ref[...])
    o_ref[...] = h + _rms(gate * e, g7_ref[...])


def _ple(h, p, gains, layer, wgate, wproj):
    n = h.shape[0]
    g6 = layer * N_NORMS + 6
    return pl.pallas_call(
        _ple_kernel,
        grid=(n // TOK_TM,),
        in_specs=[
            pl.BlockSpec((TOK_TM, D_MODEL), lambda i: (i, 0)),
            pl.BlockSpec((None, TOK_TM, PLE_DIM), lambda i: (layer, i, 0)),
            pl.BlockSpec((None, 1, D_MODEL), lambda i: (g6, 0, 0)),
            pl.BlockSpec((None, 1, D_MODEL), lambda i: (g6 + 1, 0, 0)),
            _resident((None, D_MODEL, D_MODEL), lambda i: (layer, 0, 0)),
            _resident((None, PLE_DIM, D_MODEL), lambda i: (layer, 0, 0)),
        ],
        out_specs=pl.BlockSpec((TOK_TM, D_MODEL), lambda i: (i, 0)),
        out_shape=jax.ShapeDtypeStruct(h.shape, F32),
        compiler_params=_params(("parallel",)),
        name="ple",
    )(h, p, gains, gains, wgate, wproj)


def _outproj_kernel(h_ref, a_ref, g_ref, w_ref, o_ref):
    o_ref[...] = h_ref[...] + _rms(_dot(a_ref[...], w_ref[...]), g_ref[...])


def _outproj(h, a, gains, layer, w, widx):
    n = h.shape[0]
    k = a.shape[1]
    g3 = layer * N_NORMS + 3
    return pl.pallas_call(
        _outproj_kernel,
        grid=(n // TOK_TM,),
        in_specs=[
            pl.BlockSpec((TOK_TM, D_MODEL), lambda i: (i, 0)),
            pl.BlockSpec((TOK_TM, k), lambda i: (i, 0)),
            pl.BlockSpec((None, 1, D_MODEL), lambda i: (g3, 0, 0)),
            _resident((None, k, D_MODEL), lambda i: (widx, 0, 0)),
        ],
        out_specs=pl.BlockSpec((TOK_TM, D_MODEL), lambda i: (i, 0)),
        out_shape=jax.ShapeDtypeStruct(h.shape, F32),
        compiler_params=_params(("parallel",)),
        name="outproj",
    )(h, a, gains, w)


def _rope_table_kernel(pos_ref, inv_ref, cos_ref, sin_ref):
    ang = pos_ref[...].astype(F32) * inv_ref[...]
    cos_ref[...] = jnp.cos(ang)
    sin_ref[...] = jnp.sin(ang)


def _rope_tables(positions):
    n = positions.size
    tm = 1024
    inv = ROPE_BASE ** (-jnp.arange(ROPE_HALF, dtype=F32) / ROPE_HALF)
    out = jax.ShapeDtypeStruct((n, ROPE_HALF), F32)
    return pl.pallas_call(
        _rope_table_kernel,
        grid=(n // tm,),
        in_specs=[pl.BlockSpec((tm, 1), lambda i: (i, 0)),
                  pl.BlockSpec((1, ROPE_HALF), lambda i: (0, 0))],
        out_specs=[pl.BlockSpec((tm, ROPE_HALF), lambda i: (i, 0))] * 2,
        out_shape=[out, out],
        compiler_params=_params(("parallel",)),
        name="rope_tables",
    )(positions.reshape(n, 1), inv.reshape(1, ROPE_HALF))


def _ret_proj_kernel(h_ref, g_ref, w_ref, cos_ref, sin_ref, zeta_ref,
                     q_ref, k_ref, kz_ref, v_ref, gate_ref):
    xn = _rms(h_ref[...], g_ref[...]).astype(BF16)
    cos = cos_ref[...]
    sin = sin_ref[...]
    k_scale = RET_DK ** -0.5

    def rotated(col):
        t = _dot(xn, w_ref[:, col:col + RET_DK])
        t1, t2 = t[:, :ROPE_HALF], t[:, ROPE_HALF:]
        return t1 * cos - t2 * sin, t1 * sin + t2 * cos

    for hd in range(RET_HEADS):
        lo = hd * RET_DK
        mid = lo + ROPE_HALF
        hi = lo + RET_DK
        q1, q2 = rotated(lo)
        q_ref[:, lo:mid] = q1.astype(BF16)
        q_ref[:, mid:hi] = q2.astype(BF16)
        k1, k2 = rotated(RET_QK + lo)
        k1 = k1 * k_scale
        k2 = k2 * k_scale
        k_ref[:, lo:mid] = k1.astype(BF16)
        k_ref[:, mid:hi] = k2.astype(BF16)
        kz_ref[:, lo:mid] = (k1 * zeta_ref[:, lo:mid]).astype(BF16)
        kz_ref[:, mid:hi] = (k2 * zeta_ref[:, mid:hi]).astype(BF16)
    width = 256
    for c in range(RET_V // width):
        cols = slice(c * width, (c + 1) * width)
        v_ref[:, cols] = _dot(xn, w_ref[:, 2 * RET_QK + c * width:
                                        2 * RET_QK + (c + 1) * width]).astype(BF16)
        gate_ref[:, cols] = _dot(xn, w_ref[:, 2 * RET_QK + RET_V + c * width:
                                           2 * RET_QK + RET_V + (c + 1) * width])


def _ret_proj(h, gains, layer, w_in, widx, cos, sin, zeta_tab):
    n = h.shape[0]
    g2 = layer * N_NORMS + 2
    tm = TOK_TM
    tok = lambda width: pl.BlockSpec((tm, width), lambda i: (i, 0))
    return pl.pallas_call(
        _ret_proj_kernel,
        grid=(n // tm,),
        in_specs=[
            tok(D_MODEL),
            pl.BlockSpec((None, 1, D_MODEL), lambda i: (g2, 0, 0)),
            _resident((None, D_MODEL, RET_IN), lambda i: (widx, 0, 0)),
            tok(ROPE_HALF), tok(ROPE_HALF),
            _resident((tm, RET_QK), lambda i: (0, 0)),
        ],
        out_specs=[tok(RET_QK), tok(RET_QK), tok(RET_QK), tok(RET_V), tok(RET_V)],
        out_shape=[jax.ShapeDtypeStruct((n, RET_QK), BF16)] * 3
        + [jax.ShapeDtypeStruct((n, RET_V), BF16),
           jax.ShapeDtypeStruct((n, RET_V), F32)],
        compiler_params=_params(("parallel",)),
        name="ret_proj",
    )(h, gains, w_in, cos, sin, zeta_tab)


def _ret_core_kernel(chunk_decay, q_ref, k_ref, kz_ref, v_ref, gate_ref,
                     decay_ref, xi_ref, gain_ref, o_ref, state_ref):
    @pl.when(pl.program_id(1) == 0)
    def _():
        state_ref[...] = jnp.zeros_like(state_ref)

    contract_last = (((1,), (1,)), ((), ()))
    contract_first = (((0,), (0,)), ((), ()))
    for c in range(RET_STEP // RET_CHUNK):
        rows = slice(c * RET_CHUNK, (c + 1) * RET_CHUNK)
        for hd in range(RET_HEADS):
            qk_cols = slice(hd * RET_DK, (hd + 1) * RET_DK)
            v_cols = slice(hd * RET_DV, (hd + 1) * RET_DV)
            q = q_ref[rows, qk_cols]
            v = v_ref[rows, v_cols]
            scores = lax.dot_general(q, k_ref[rows, qk_cols], contract_last,
                                     preferred_element_type=F32) * decay_ref[hd]
            state = state_ref[hd]
            o = _dot(scores.astype(BF16), v) + _dot(q, state.astype(BF16)) * xi_ref[hd]
            state_ref[hd] = state * chunk_decay[hd] + lax.dot_general(
                kz_ref[rows, qk_cols], v, contract_first, preferred_element_type=F32)
            mu = jnp.mean(o, axis=-1, keepdims=True)
            d = o - mu
            var = jnp.mean(d * d, axis=-1, keepdims=True)
            normed = d * lax.rsqrt(var + GN_EPS) * gain_ref[:, v_cols]
            o_ref[rows, v_cols] = (_silu(gate_ref[rows, v_cols]) * normed).astype(BF16)


def _ret_core(q, k, kz, v, gate, decay, xi, gn_gain, chunk_decay, batch, seq):
    n = q.shape[0]
    steps = seq // RET_STEP
    tok = lambda width: pl.BlockSpec((RET_STEP, width), lambda b, t: (b * steps + t, 0))
    return pl.pallas_call(
        functools.partial(_ret_core_kernel, chunk_decay),
        grid=(batch, steps),
        in_specs=[
            tok(RET_QK), tok(RET_QK), tok(RET_QK), tok(RET_V), tok(RET_V),
            _resident((RET_HEADS, RET_CHUNK, RET_CHUNK), lambda b, t: (0, 0, 0)),
            _resident((RET_HEADS, RET_CHUNK, RET_DV), lambda b, t: (0, 0, 0)),
            pl.BlockSpec((1, RET_V), lambda b, t: (0, 0)),
        ],
        out_specs=tok(RET_V),
        out_shape=jax.ShapeDtypeStruct((n, RET_V), BF16),
        scratch_shapes=[pltpu.VMEM((RET_HEADS, RET_DK, RET_DV), F32)],
        compiler_params=_params(("parallel", "arbitrary")),
        name="ret_core",
    )(q, k, kz, v, gate, decay, xi, gn_gain)


def _retention_constants():
    heads = jnp.arange(RET_HEADS, dtype=F32)
    log_gamma = jnp.log1p(-jnp.exp2(-5.0 - heads))
    idx = jnp.arange(RET_CHUNK, dtype=F32)
    rel = idx[:, None] - idx[None, :]
    inner = jnp.where(rel[None] >= 0,
                      jnp.exp(jnp.maximum(rel, 0.0)[None] * log_gamma[:, None, None]), 0.0)
    xi = jnp.exp((idx + 1.0)[None, :] * log_gamma[:, None])
    zeta = jnp.exp((RET_CHUNK - 1.0 - idx)[None, :] * log_gamma[:, None])
    return inner, xi, zeta


def _chunk_decay():
    return tuple(float((1.0 - 2.0 ** (-5 - hd)) ** RET_CHUNK) for hd in range(RET_HEADS))


def _sb_proj_kernel(h_ref, g_ref, w_ref, o_ref):
    xn = _rms(h_ref[...], g_ref[...]).astype(BF16)
    width = 512
    for c in range(SB_IN // width):
        cols = slice(c * width, (c + 1) * width)
        o_ref[:, cols] = _dot(xn, w_ref[:, cols]).astype(BF16)


def _sb_proj(h, gains, layer, w_in, widx):
    n = h.shape[0]
    g2 = layer * N_NORMS + 2
    return pl.pallas_call(
        _sb_proj_kernel,
        grid=(n // TOK_TM,),
        in_specs=[
            pl.BlockSpec((TOK_TM, D_MODEL), lambda i: (i, 0)),
            pl.BlockSpec((None, 1, D_MODEL), lambda i: (g2, 0, 0)),
            _resident((None, D_MODEL, SB_IN), lambda i: (widx, 0, 0)),
        ],
        out_specs=pl.BlockSpec((TOK_TM, SB_IN), lambda i: (i, 0)),
        out_shape=jax.ShapeDtypeStruct((n, SB_IN), BF16),
        compiler_params=_params(("parallel",)),
        name="sb_proj",
    )(h, gains, w_in)


def _sb_core_kernel(q_ref, k_ref, v_ref, o_ref):
    blk = SB_BLOCK
    seq = q_ref.shape[0]
    scale = SB_DH ** -0.5
    row = lax.broadcasted_iota(jnp.int32, (blk, blk), 0)
    col = lax.broadcasted_iota(jnp.int32, (blk, blk), 1)
    causal = col < row
    r2 = lax.broadcasted_iota(jnp.int32, (blk, 2 * blk), 0)
    c2 = lax.broadcasted_iota(jnp.int32, (blk, 2 * blk), 1)
    suffix_ones = jnp.where((r2 >= c2) | (c2 >= blk), 1.0, 0.0).astype(BF16)
    contract_last = (((1,), (1,)), ((), ()))

    def tile(q, k0, rest, mask):
        k = k_ref[pl.ds(k0, blk), :]
        v = v_ref[pl.ds(k0, blk), :]
        z = lax.dot_general(q, k, contract_last, preferred_element_type=F32) * scale
        log_rest = -(jnp.maximum(z, 0.0) + jnp.log(1.0 + jnp.exp(-jnp.abs(z))))
        if mask is not None:
            log_rest = jnp.where(mask, log_rest, 0.0)
        hi = log_rest.astype(BF16)
        lo = (log_rest - hi.astype(F32)).astype(BF16)
        sums = _dot(jnp.concatenate([hi, lo], axis=0), suffix_ones)
        sums = sums[:blk] + sums[blk:]
        a = jnp.exp(z + sums[:, :blk] + rest)
        if mask is not None:
            a = jnp.where(mask, a, 0.0)
        return _dot(a.astype(BF16), v), rest + sums[:, blk:]

    def q_block(qi, carry):
        q0 = pl.multiple_of(qi * blk, blk)
        q = q_ref[pl.ds(q0, blk), :]
        acc, rest = tile(q, q0, jnp.zeros((blk, blk), F32), causal)

        def more(c):
            j, _, _, worst = c
            return jnp.logical_and(j >= 0, worst > SB_SKIP_LOG)

        def step(c):
            j, acc, rest, _ = c
            part, rest = tile(q, pl.multiple_of(j * blk, blk), rest, None)
            return j - 1, acc + part, rest, jnp.max(rest)

        _, acc, _, _ = lax.while_loop(more, step, (qi - 1, acc, rest, jnp.max(rest)))
        o_ref[pl.ds(q0, blk), :] = acc.astype(BF16)
        return carry

    lax.fori_loop(0, seq // blk, q_block, 0)


def _sb_core(qkv, batch, seq):
    qkv = qkv.reshape(batch, seq, SB_IN)
    head = lambda part: pl.BlockSpec((None, seq, SB_DH),
                                     lambda b, hd: (b, 0, part * SB_HEADS + hd))
    out = pl.pallas_call(
        _sb_core_kernel,
        grid=(batch, SB_HEADS),
        in_specs=[head(0), head(1), head(2)],
        out_specs=head(0),
        out_shape=jax.ShapeDtypeStruct((batch, seq, SB_WIDTH), BF16),
        compiler_params=_params(("parallel", "parallel")),
        name="sb_core",
    )(qkv, qkv, qkv)
    return out.reshape(batch * seq, SB_WIDTH)


def kernel(x, p, positions, norm_gains, ffn_w_gate, ffn_w_up, ffn_w_down, ret_w_in,
           ret_gn_gain, ret_w_out, sb_w_in, sb_w_out, ple_w_gate, ple_w_proj):
    batch, seq, _ = x.shape
    n = batch * seq
    h = x.reshape(n, D_MODEL)
    p = p.reshape(DEPTH, n, PLE_DIM)
    gains = norm_gains.reshape(DEPTH * N_NORMS, 1, D_MODEL)
    wg, wu, wd = (w.astype(BF16) for w in (ffn_w_gate, ffn_w_up, ffn_w_down))
    ret_w_in, ret_w_out, sb_w_in, sb_w_out, ple_w_gate, ple_w_proj = (
        w.astype(BF16) for w in (ret_w_in, ret_w_out, sb_w_in, sb_w_out,
                                 ple_w_gate, ple_w_proj))

    cos, sin = _rope_tables(positions)
    inner, xi, zeta = _retention_constants()
    xi_tab = jnp.broadcast_to(xi[:, :, None], (RET_HEADS, RET_CHUNK, RET_DV))
    zeta_tab = jnp.tile(jnp.repeat(zeta.T, RET_DK, axis=1), (TOK_TM // RET_CHUNK, 1))
    chunk_decay = _chunk_decay()

    for layer in range(DEPTH):
        j = layer // N_MIXERS
        h = _ffn(h, gains, layer, 0, wg, wu, wd)
        if layer % N_MIXERS == 0:
            q, k, kz, v, gate = _ret_proj(h, gains, layer, ret_w_in, j, cos, sin, zeta_tab)
            mixed = _ret_core(q, k, kz, v, gate, inner, xi_tab,
                              ret_gn_gain[j].reshape(1, RET_V), chunk_decay, batch, seq)
            h = _outproj(h, mixed, gains, layer, ret_w_out, j)
        else:
            qkv = _sb_proj(h, gains, layer, sb_w_in, j)
            mixed = _sb_core(qkv, batch, seq)
            h = _outproj(h, mixed, gains, layer, sb_w_out, j)
        h = _ffn(h, gains, layer, 1, wg, wu, wd)
        h = _ple(h, p, gains, layer, ple_w_gate, ple_w_proj)
    return h.reshape(batch, seq, D_MODEL)
```

```python
import functools

import jax
import jax.numpy as jnp
from jax import lax
from jax.experimental import pallas as pl
from jax.experimental.pallas import tpu as pltpu

F32 = jnp.float32
BF16 = jnp.bfloat16

D_MODEL = 1024
DEPTH = 4
N_MIXERS = 2
PLE_DIM = 256
D_FF = 2816
FFN_RES_WEIGHT = 0.5
RET_HEADS = 4
RET_DK = D_MODEL // RET_HEADS
RET_QK = RET_HEADS * RET_DK
RET_DV = 2 * RET_DK
RET_V = RET_HEADS * RET_DV
RET_IN = 2 * RET_QK + 2 * RET_V
RET_CHUNK = 128
ROPE_BASE = 10000.0
ROPE_HALF = RET_DK // 2
GN_EPS = 1e-5
SB_HEADS = 8
SB_DH = D_MODEL // SB_HEADS
SB_WIDTH = SB_HEADS * SB_DH
SB_IN = 3 * SB_WIDTH
SB_BLOCK = 128
N_NORMS = 8
RMS_EPS = 1e-6

VMEM_LIMIT_BYTES = 56 * 1024 * 1024

FFN_TM = 512
FFN_TF = 256
SB_GROUP = 4
TOK_TM = 512
RET_STEP = 512

SB_SKIP_LOG = -105.0


def _params(sem):
    return pltpu.CompilerParams(dimension_semantics=sem,
                                vmem_limit_bytes=VMEM_LIMIT_BYTES)


def _resident(shape, index_map):
    return pl.BlockSpec(shape, index_map, pipeline_mode=pl.Buffered(1))


def _rms(x, g):
    ms = jnp.mean(x * x, axis=-1, keepdims=True)
    return x * lax.rsqrt(ms + RMS_EPS) * g


def _dot(a, b):
    return jnp.dot(a, b, preferred_element_type=F32)


def _silu(x):
    return x * jax.nn.sigmoid(x)


def _ffn_kernel(h_ref, gpre_ref, gpost_ref, wg_ref, wu_ref, wd_ref, o_ref,
                xn_ref, hid_ref):
    xn_ref[...] = _rms(h_ref[...], gpre_ref[...]).astype(BF16)
    for c in range(D_FF // FFN_TF):
        cols = slice(c * FFN_TF, (c + 1) * FFN_TF)
        xn = xn_ref[...]
        hid_ref[:, cols] = (_silu(_dot(xn, wg_ref[:, cols]))
                            * _dot(xn, wu_ref[:, cols])).astype(BF16)
    f = _dot(hid_ref[...], wd_ref[...])
    o_ref[...] = h_ref[...] + FFN_RES_WEIGHT * _rms(f, gpost_ref[...])


def _ffn(h, gains, layer, which, wg, wu, wd):
    n = h.shape[0]
    g_pre = layer * N_NORMS + (0 if which == 0 else 4)
    g_post = g_pre + 1
    return pl.pallas_call(
        _ffn_kernel,
        grid=(n // FFN_TM,),
        in_specs=[
            pl.BlockSpec((FFN_TM, D_MODEL), lambda i: (i, 0)),
            pl.BlockSpec((None, 1, D_MODEL), lambda i: (g_pre, 0, 0)),
            pl.BlockSpec((None, 1, D_MODEL), lambda i: (g_post, 0, 0)),
            _resident((None, None, D_MODEL, D_FF), lambda i: (layer, which, 0, 0)),
            _resident((None, None, D_MODEL, D_FF), lambda i: (layer, which, 0, 0)),
            _resident((None, None, D_FF, D_MODEL), lambda i: (layer, which, 0, 0)),
        ],
        out_specs=pl.BlockSpec((FFN_TM, D_MODEL), lambda i: (i, 0)),
        out_shape=jax.ShapeDtypeStruct(h.shape, F32),
        scratch_shapes=[pltpu.VMEM((FFN_TM, D_MODEL), BF16),
                        pltpu.VMEM((FFN_TM, D_FF), BF16)],
        compiler_params=_params(("parallel",)),
        name="ffn",
    )(h, gains, gains, wg, wu, wd)


def _ple_kernel(h_ref, p_ref, g6_ref, g7_ref, wgate_ref, wproj_ref, o_ref):
    h = h_ref[...]
    xn = _rms(h, g6_ref[...]).astype(BF16)
    gate = jax.nn.sigmoid(_dot(xn, wgate_ref[...]))
    e = _dot(p_ref[...].astype(BF16), wproj_ref[...])
    o_ref[...] = h + _rms(gate * e, g7_ref[...])


def _ple(h, p, gains, layer, wgate, wproj):
    n = h.shape[0]
    g6 = layer * N_NORMS + 6
    return pl.pallas_call(
        _ple_kernel,
        grid=(n // TOK_TM,),
        in_specs=[
            pl.BlockSpec((TOK_TM, D_MODEL), lambda i: (i, 0)),
            pl.BlockSpec((None, TOK_TM, PLE_DIM), lambda i: (layer, i, 0)),
            pl.BlockSpec((None, 1, D_MODEL), lambda i: (g6, 0, 0)),
            pl.BlockSpec((None, 1, D_MODEL), lambda i: (g6 + 1, 0, 0)),
            _resident((None, D_MODEL, D_MODEL), lambda i: (layer, 0, 0)),
            _resident((None, PLE_DIM, D_MODEL), lambda i: (layer, 0, 0)),
        ],
        out_specs=pl.BlockSpec((TOK_TM, D_MODEL), lambda i: (i, 0)),
        out_shape=jax.ShapeDtypeStruct(h.shape, F32),
        compiler_params=_params(("parallel",)),
        name="ple",
    )(h, p, gains, gains, wgate, wproj)


def _outproj_kernel(h_ref, a_ref, g_ref, w_ref, o_ref):
    o_ref[...] = h_ref[...] + _rms(_dot(a_ref[...], w_ref[...]), g_ref[...])


def _outproj(h, a, gains, layer, w, widx):
    n = h.shape[0]
    k = a.shape[1]
    g3 = layer * N_NORMS + 3
    return pl.pallas_call(
        _outproj_kernel,
        grid=(n // TOK_TM,),
        in_specs=[
            pl.BlockSpec((TOK_TM, D_MODEL), lambda i: (i, 0)),
            pl.BlockSpec((TOK_TM, k), lambda i: (i, 0)),
            pl.BlockSpec((None, 1, D_MODEL), lambda i: (g3, 0, 0)),
            _resident((None, k, D_MODEL), lambda i: (widx, 0, 0)),
        ],
        out_specs=pl.BlockSpec((TOK_TM, D_MODEL), lambda i: (i, 0)),
        out_shape=jax.ShapeDtypeStruct(h.shape, F32),
        compiler_params=_params(("parallel",)),
        name="outproj",
    )(h, a, gains, w)


def _rope_table_kernel(pos_ref, inv_ref, cos_ref, sin_ref):
    ang = pos_ref[...].astype(F32) * inv_ref[...]
    cos_ref[...] = jnp.cos(ang)
    sin_ref[...] = jnp.sin(ang)


def _rope_tables(positions):
    n = positions.size
    tm = 1024
    inv = ROPE_BASE ** (-jnp.arange(ROPE_HALF, dtype=F32) / ROPE_HALF)
    out = jax.ShapeDtypeStruct((n, ROPE_HALF), F32)
    return pl.pallas_call(
        _rope_table_kernel,
        grid=(n // tm,),
        in_specs=[pl.BlockSpec((tm, 1), lambda i: (i, 0)),
                  pl.BlockSpec((1, ROPE_HALF), lambda i: (0, 0))],
        out_specs=[pl.BlockSpec((tm, ROPE_HALF), lambda i: (i, 0))] * 2,
        out_shape=[out, out],
        compiler_params=_params(("parallel",)),
        name="rope_tables",
    )(positions.reshape(n, 1), inv.reshape(1, ROPE_HALF))


def _ret_proj_kernel(h_ref, g_ref, w_ref, cos_ref, sin_ref, zeta_ref,
                     q_ref, k_ref, kz_ref, v_ref, gate_ref):
    xn = _rms(h_ref[...], g_ref[...]).astype(BF16)
    cos = cos_ref[...]
    sin = sin_ref[...]
    k_scale = RET_DK ** -0.5

    def rotated(col):
        t = _dot(xn, w_ref[:, col:col + RET_DK])
        t1, t2 = t[:, :ROPE_HALF], t[:, ROPE_HALF:]
        return t1 * cos - t2 * sin, t1 * sin + t2 * cos

    for hd in range(RET_HEADS):
        lo = hd * RET_DK
        mid = lo + ROPE_HALF
        hi = lo + RET_DK
        q1, q2 = rotated(lo)
        q_ref[:, lo:mid] = q1.astype(BF16)
        q_ref[:, mid:hi] = q2.astype(BF16)
        k1, k2 = rotated(RET_QK + lo)
        k1 = k1 * k_scale
        k2 = k2 * k_scale
        k_ref[:, lo:mid] = k1.astype(BF16)
        k_ref[:, mid:hi] = k2.astype(BF16)
        kz_ref[:, lo:mid] = (k1 * zeta_ref[:, lo:mid]).astype(BF16)
        kz_ref[:, mid:hi] = (k2 * zeta_ref[:, mid:hi]).astype(BF16)
    width = 256
    for c in range(RET_V // width):
        cols = slice(c * width, (c + 1) * width)
        v_ref[:, cols] = _dot(xn, w_ref[:, 2 * RET_QK + c * width:
                                        2 * RET_QK + (c + 1) * width]).astype(BF16)
        gate_ref[:, cols] = _dot(xn, w_ref[:, 2 * RET_QK + RET_V + c * width:
                                           2 * RET_QK + RET_V + (c + 1) * width])


def _ret_proj(h, gains, layer, w_in, widx, cos, sin, zeta_tab):
    n = h.shape[0]
    g2 = layer * N_NORMS + 2
    tm = TOK_TM
    tok = lambda width: pl.BlockSpec((tm, width), lambda i: (i, 0))
    return pl.pallas_call(
        _ret_proj_kernel,
        grid=(n // tm,),
        in_specs=[
            tok(D_MODEL),
            pl.BlockSpec((None, 1, D_MODEL), lambda i: (g2, 0, 0)),
            _resident((None, D_MODEL, RET_IN), lambda i: (widx, 0, 0)),
            tok(ROPE_HALF), tok(ROPE_HALF),
            _resident((tm, RET_QK), lambda i: (0, 0)),
        ],
        out_specs=[tok(RET_QK), tok(RET_QK), tok(RET_QK), tok(RET_V), tok(RET_V)],
        out_shape=[jax.ShapeDtypeStruct((n, RET_QK), BF16)] * 3
        + [jax.ShapeDtypeStruct((n, RET_V), BF16),
           jax.ShapeDtypeStruct((n, RET_V), F32)],
        compiler_params=_params(("parallel",)),
        name="ret_proj",
    )(h, gains, w_in, cos, sin, zeta_tab)


def _ret_core_kernel(chunk_decay, q_ref, k_ref, kz_ref, v_ref, gate_ref,
                     decay_ref, xi_ref, gain_ref, o_ref, state_ref):
    @pl.when(pl.program_id(1) == 0)
    def _():
        state_ref[...] = jnp.zeros_like(state_ref)

    contract_last = (((1,), (1,)), ((), ()))
    contract_first = (((0,), (0,)), ((), ()))
    for c in range(RET_STEP // RET_CHUNK):
        rows = slice(c * RET_CHUNK, (c + 1) * RET_CHUNK)
        for hd in range(RET_HEADS):
            qk_cols = slice(hd * RET_DK, (hd + 1) * RET_DK)
            v_cols = slice(hd * RET_DV, (hd + 1) * RET_DV)
            q = q_ref[rows, qk_cols]
            v = v_ref[rows, v_cols]
            scores = lax.dot_general(q, k_ref[rows, qk_cols], contract_last,
                                     preferred_element_type=F32) * decay_ref[hd]
            state = state_ref[hd]
            o = _dot(scores.astype(BF16), v) + _dot(q, state.astype(BF16)) * xi_ref[hd]
            state_ref[hd] = state * chunk_decay[hd] + lax.dot_general(
                kz_ref[rows, qk_cols], v, contract_first, preferred_element_type=F32)
            mu = jnp.mean(o, axis=-1, keepdims=True)
            d = o - mu
            var = jnp.mean(d * d, axis=-1, keepdims=True)
            normed = d * lax.rsqrt(var + GN_EPS) * gain_ref[:, v_cols]
            o_ref[rows, v_cols] = (_silu(gate_ref[rows, v_cols]) * normed).astype(BF16)


def _ret_core(q, k, kz, v, gate, decay, xi, gn_gain, chunk_decay, batch, seq):
    n = q.shape[0]
    steps = seq // RET_STEP
    tok = lambda width: pl.BlockSpec((RET_STEP, width), lambda b, t: (b * steps + t, 0))
    return pl.pallas_call(
        functools.partial(_ret_core_kernel, chunk_decay),
        grid=(batch, steps),
        in_specs=[
            tok(RET_QK), tok(RET_QK), tok(RET_QK), tok(RET_V), tok(RET_V),
            _resident((RET_HEADS, RET_CHUNK, RET_CHUNK), lambda b, t: (0, 0, 0)),
            _resident((RET_HEADS, RET_CHUNK, RET_DV), lambda b, t: (0, 0, 0)),
            pl.BlockSpec((1, RET_V), lambda b, t: (0, 0)),
        ],
        out_specs=tok(RET_V),
        out_shape=jax.ShapeDtypeStruct((n, RET_V), BF16),
        scratch_shapes=[pltpu.VMEM((RET_HEADS, RET_DK, RET_DV), F32)],
        compiler_params=_params(("parallel", "arbitrary")),
        name="ret_core",
    )(q, k, kz, v, gate, decay, xi, gn_gain)


def _retention_constants():
    heads = jnp.arange(RET_HEADS, dtype=F32)
    log_gamma = jnp.log1p(-jnp.exp2(-5.0 - heads))
    idx = jnp.arange(RET_CHUNK, dtype=F32)
    rel = idx[:, None] - idx[None, :]
    inner = jnp.where(rel[None] >= 0,
                      jnp.exp(jnp.maximum(rel, 0.0)[None] * log_gamma[:, None, None]), 0.0)
    xi = jnp.exp((idx + 1.0)[None, :] * log_gamma[:, None])
    zeta = jnp.exp((RET_CHUNK - 1.0 - idx)[None, :] * log_gamma[:, None])
    return inner, xi, zeta


def _chunk_decay():
    return tuple(float((1.0 - 2.0 ** (-5 - hd)) ** RET_CHUNK) for hd in range(RET_HEADS))


def _sb_proj_kernel(h_ref, g_ref, w_ref, o_ref):
    xn = _rms(h_ref[...], g_ref[...]).astype(BF16)
    width = 512
    for c in range(SB_IN // width):
        cols = slice(c * width, (c + 1) * width)
        o_ref[:, cols] = _dot(xn, w_ref[:, cols]).astype(BF16)


def _sb_proj(h, gains, layer, w_in, widx):
    n = h.shape[0]
    g2 = layer * N_NORMS + 2
    return pl.pallas_call(
        _sb_proj_kernel,
        grid=(n // TOK_TM,),
        in_specs=[
            pl.BlockSpec((TOK_TM, D_MODEL), lambda i: (i, 0)),
            pl.BlockSpec((None, 1, D_MODEL), lambda i: (g2, 0, 0)),
            _resident((None, D_MODEL, SB_IN), lambda i: (widx, 0, 0)),
        ],
        out_specs=pl.BlockSpec((TOK_TM, SB_IN), lambda i: (i, 0)),
        out_shape=jax.ShapeDtypeStruct((n, SB_IN), BF16),
        compiler_params=_params(("parallel",)),
        name="sb_proj",
    )(h, gains, w_in)


def _sb_core_kernel(q_ref, k_ref, v_ref, o_ref, acc_ref, rest_ref):
    blk = SB_BLOCK
    seq = q_ref.shape[0]
    scale = SB_DH ** -0.5
    row = lax.broadcasted_iota(jnp.int32, (blk, blk), 0)
    col = lax.broadcasted_iota(jnp.int32, (blk, blk), 1)
    causal = col < row
    r2 = lax.broadcasted_iota(jnp.int32, (blk, 2 * blk), 0)
    c2 = lax.broadcasted_iota(jnp.int32, (blk, 2 * blk), 1)
    suffix_ones = jnp.where((r2 >= c2) | (c2 >= blk), 1.0, 0.0).astype(BF16)
    contract_last = (((1,), (1,)), ((), ()))

    def tiles(q0, k0, diagonal):
        heads = range(SB_GROUP)
        cols = [slice(g * SB_DH, (g + 1) * SB_DH) for g in heads]
        zs = [lax.dot_general(q_ref[pl.ds(q0, blk), cols[g]], k_ref[pl.ds(k0, blk), cols[g]],
                              contract_last, preferred_element_type=F32) * scale
              for g in heads]
        sums = []
        for g in heads:
            z = zs[g]
            log_rest = -(jnp.maximum(z, 0.0) + jnp.log(1.0 + jnp.exp(-jnp.abs(z))))
            if diagonal:
                log_rest = jnp.where(causal, log_rest, 0.0)
            hi = log_rest.astype(BF16)
            lo = (log_rest - hi.astype(F32)).astype(BF16)
            both = _dot(jnp.concatenate([hi, lo], axis=0), suffix_ones)
            sums.append(both[:blk] + both[blk:])
        worst = None
        for g in heads:
            if diagonal:
                a = jnp.where(causal, jnp.exp(zs[g] + sums[g][:, :blk]), 0.0)
                rest = sums[g][:, blk:]
            else:
                a = jnp.exp(zs[g] + sums[g][:, :blk] + rest_ref[g])
                rest = rest_ref[g] + sums[g][:, blk:]
            part = _dot(a.astype(BF16), v_ref[pl.ds(k0, blk), cols[g]])
            if diagonal:
                acc_ref[g] = part
            else:
                acc_ref[g] += part
            rest_ref[g] = rest
            worst = rest if worst is None else jnp.maximum(worst, rest)
        return jnp.max(worst)

    def q_block(qi, carry):
        q0 = pl.multiple_of(qi * blk, blk)

        def more(c):
            j, worst = c
            return jnp.logical_and(j >= 0, worst > SB_SKIP_LOG)

        def step(c):
            j, _ = c
            return j - 1, tiles(q0, pl.multiple_of(j * blk, blk), False)

        lax.while_loop(more, step, (qi - 1, tiles(q0, q0, True)))
        for g in range(SB_GROUP):
            o_ref[pl.ds(q0, blk), g * SB_DH:(g + 1) * SB_DH] = acc_ref[g].astype(BF16)
        return carry

    lax.fori_loop(0, seq // blk, q_block, 0)


def _sb_core(qkv, batch, seq):
    qkv = qkv.reshape(batch, seq, SB_IN)
    groups = SB_HEADS // SB_GROUP
    width = SB_GROUP * SB_DH
    heads = lambda part: pl.BlockSpec((None, seq, width),
                                      lambda b, g: (b, 0, part * groups + g))
    out = pl.pallas_call(
        _sb_core_kernel,
        grid=(batch, groups),
        in_specs=[heads(0), heads(1), heads(2)],
        out_specs=heads(0),
        out_shape=jax.ShapeDtypeStruct((batch, seq, SB_WIDTH), BF16),
        scratch_shapes=[pltpu.VMEM((SB_GROUP, SB_BLOCK, SB_DH), F32),
                        pltpu.VMEM((SB_GROUP, SB_BLOCK, SB_BLOCK), F32)],
        compiler_params=_params(("parallel", "parallel")),
        name="sb_core",
    )(qkv, qkv, qkv)
    return out.reshape(batch * seq, SB_WIDTH)


def kernel(x, p, positions, norm_gains, ffn_w_gate, ffn_w_up, ffn_w_down, ret_w_in,
           ret_gn_gain, ret_w_out, sb_w_in, sb_w_out, ple_w_gate, ple_w_proj):
    batch, seq, _ = x.shape
    n = batch * seq
    h = x.reshape(n, D_MODEL)
    p = p.reshape(DEPTH, n, PLE_DIM)
    gains = norm_gains.reshape(DEPTH * N_NORMS, 1, D_MODEL)
    wg, wu, wd = (w.astype(BF16) for w in (ffn_w_gate, ffn_w_up, ffn_w_down))
    ret_w_in, ret_w_out, sb_w_in, sb_w_out, ple_w_gate, ple_w_proj = (
        w.astype(BF16) for w in (ret_w_in, ret_w_out, sb_w_in, sb_w_out,
                                 ple_w_gate, ple_w_proj))

    cos, sin = _rope_tables(positions)
    inner, xi, zeta = _retention_constants()
    xi_tab = jnp.broadcast_to(xi[:, :, None], (RET_HEADS, RET_CHUNK, RET_DV))
    zeta_tab = jnp.tile(jnp.repeat(zeta.T, RET_DK, axis=1), (TOK_TM // RET_CHUNK, 1))
    chunk_decay = _chunk_decay()

    for layer in range(DEPTH):
        j = layer // N_MIXERS
        h = _ffn(h, gains, layer, 0, wg, wu, wd)
        if layer % N_MIXERS == 0:
            q, k, kz, v, gate = _ret_proj(h, gains, layer, ret_w_in, j, cos, sin, zeta_tab)
            mixed = _ret_core(q, k, kz, v, gate, inner, xi_tab,
                              ret_gn_gain[j].reshape(1, RET_V), chunk_decay, batch, seq)
            h = _outproj(h, mixed, gains, layer, ret_w_out, j)
        else:
            qkv = _sb_proj(h, gains, layer, sb_w_in, j)
            mixed = _sb_core(qkv, batch, seq)
            h = _outproj(h, mixed, gains, layer, sb_w_out, j)
        h = _ffn(h, gains, layer, 1, wg, wu, wd)
        h = _ple(h, p, gains, layer, ple_w_gate, ple_w_proj)
    return h.reshape(batch, seq, D_MODEL)
```

```python
import functools

import jax
import jax.numpy as jnp
from jax import lax
from jax.experimental import pallas as pl
from jax.experimental.pallas import tpu as pltpu

F32 = jnp.float32
BF16 = jnp.bfloat16

D_MODEL = 1024
DEPTH = 4
N_MIXERS = 2
PLE_DIM = 256
D_FF = 2816
FFN_RES_WEIGHT = 0.5
RET_HEADS = 4
RET_DK = D_MODEL // RET_HEADS
RET_QK = RET_HEADS * RET_DK
RET_DV = 2 * RET_DK
RET_V = RET_HEADS * RET_DV
RET_IN = 2 * RET_QK + 2 * RET_V
RET_CHUNK = 128
ROPE_BASE = 10000.0
ROPE_HALF = RET_DK // 2
GN_EPS = 1e-5
SB_HEADS = 8
SB_DH = D_MODEL // SB_HEADS
SB_WIDTH = SB_HEADS * SB_DH
SB_IN = 3 * SB_WIDTH
SB_BLOCK = 128
N_NORMS = 8
RMS_EPS = 1e-6
LOG2_E = 1.4426950408889634

VMEM_LIMIT_BYTES = 56 * 1024 * 1024

FFN_TM = 512
FFN_TF = 256
SB_GROUP = 4
SB_FIRST_PASS = 3
SB_QBLOCKS = 1
TOK_TM = 512
RET_STEP = 512

SB_SKIP_LOG = -105.0


def _params(sem):
    return pltpu.CompilerParams(dimension_semantics=sem,
                                vmem_limit_bytes=VMEM_LIMIT_BYTES)


def _resident(shape, index_map):
    return pl.BlockSpec(shape, index_map, pipeline_mode=pl.Buffered(1))


def _rms(x, g):
    ms = jnp.mean(x * x, axis=-1, keepdims=True)
    return x * lax.rsqrt(ms + RMS_EPS) * g


def _dot(a, b):
    return jnp.dot(a, b, preferred_element_type=F32)


def _silu(x):
    return x * jax.nn.sigmoid(x)


def _ffn_kernel(h_ref, gpre_ref, gpost_ref, wg_ref, wu_ref, wd_ref, o_ref,
                xn_ref, hid_ref):
    xn_ref[...] = _rms(h_ref[...], gpre_ref[...]).astype(BF16)
    for c in range(D_FF // FFN_TF):
        cols = slice(c * FFN_TF, (c + 1) * FFN_TF)
        xn = xn_ref[...]
        hid_ref[:, cols] = (_silu(_dot(xn, wg_ref[:, cols]))
                            * _dot(xn, wu_ref[:, cols])).astype(BF16)
    f = _dot(hid_ref[...], wd_ref[...])
    o_ref[...] = h_ref[...] + FFN_RES_WEIGHT * _rms(f, gpost_ref[...])


def _ffn(h, gains, layer, which, wg, wu, wd):
    n = h.shape[0]
    g_pre = layer * N_NORMS + (0 if which == 0 else 4)
    g_post = g_pre + 1
    return pl.pallas_call(
        _ffn_kernel,
        grid=(n // FFN_TM,),
        in_specs=[
            pl.BlockSpec((FFN_TM, D_MODEL), lambda i: (i, 0)),
            pl.BlockSpec((None, 1, D_MODEL), lambda i: (g_pre, 0, 0)),
            pl.BlockSpec((None, 1, D_MODEL), lambda i: (g_post, 0, 0)),
            _resident((None, None, D_MODEL, D_FF), lambda i: (layer, which, 0, 0)),
            _resident((None, None, D_MODEL, D_FF), lambda i: (layer, which, 0, 0)),
            _resident((None, None, D_FF, D_MODEL), lambda i: (layer, which, 0, 0)),
        ],
        out_specs=pl.BlockSpec((FFN_TM, D_MODEL), lambda i: (i, 0)),
        out_shape=jax.ShapeDtypeStruct(h.shape, F32),
        scratch_shapes=[pltpu.VMEM((FFN_TM, D_MODEL), BF16),
                        pltpu.VMEM((FFN_TM, D_FF), BF16)],
        compiler_params=_params(("parallel",)),
        name="ffn",
    )(h, gains, gains, wg, wu, wd)


def _ple_kernel(h_ref, p_ref, g6_ref, g7_ref, wgate_ref, wproj_ref, o_ref):
    h = h_ref[...]
    xn = _rms(h, g6_ref[...]).astype(BF16)
    gate = jax.nn.sigmoid(_dot(xn, wgate_ref[...]))
    e = _dot(p_ref[...].astype(BF16), wproj_ref[...])
    o_ref[...] = h + _rms(gate * e, g7_ref[...])


def _ple(h, p, gains, layer, wgate, wproj):
    n = h.shape[0]
    g6 = layer * N_NORMS + 6
    return pl.pallas_call(
        _ple_kernel,
        grid=(n // TOK_TM,),
        in_specs=[
            pl.BlockSpec((TOK_TM, D_MODEL), lambda i: (i, 0)),
            pl.BlockSpec((None, TOK_TM, PLE_DIM), lambda i: (layer, i, 0)),
            pl.BlockSpec((None, 1, D_MODEL), lambda i: (g6, 0, 0)),
            pl.BlockSpec((None, 1, D_MODEL), lambda i: (g6 + 1, 0, 0)),
            _resident((None, D_MODEL, D_MODEL), lambda i: (layer, 0, 0)),
            _resident((None, PLE_DIM, D_MODEL), lambda i: (layer, 0, 0)),
        ],
        out_specs=pl.BlockSpec((TOK_TM, D_MODEL), lambda i: (i, 0)),
        out_shape=jax.ShapeDtypeStruct(h.shape, F32),
        compiler_params=_params(("parallel",)),
        name="ple",
    )(h, p, gains, gains, wgate, wproj)


def _outproj_kernel(h_ref, a_ref, g_ref, w_ref, o_ref):
    o_ref[...] = h_ref[...] + _rms(_dot(a_ref[...], w_ref[...]), g_ref[...])


def _outproj(h, a, gains, layer, w, widx):
    n = h.shape[0]
    k = a.shape[1]
    g3 = layer * N_NORMS + 3
    return pl.pallas_call(
        _outproj_kernel,
        grid=(n // TOK_TM,),
        in_specs=[
            pl.BlockSpec((TOK_TM, D_MODEL), lambda i: (i, 0)),
            pl.BlockSpec((TOK_TM, k), lambda i: (i, 0)),
            pl.BlockSpec((None, 1, D_MODEL), lambda i: (g3, 0, 0)),
            _resident((None, k, D_MODEL), lambda i: (widx, 0, 0)),
        ],
        out_specs=pl.BlockSpec((TOK_TM, D_MODEL), lambda i: (i, 0)),
        out_shape=jax.ShapeDtypeStruct(h.shape, F32),
        compiler_params=_params(("parallel",)),
        name="outproj",
    )(h, a, gains, w)


def _rope_table_kernel(pos_ref, inv_ref, cos_ref, sin_ref):
    ang = pos_ref[...].astype(F32) * inv_ref[...]
    cos_ref[...] = jnp.cos(ang)
    sin_ref[...] = jnp.sin(ang)


def _rope_tables(positions):
    n = positions.size
    tm = 1024
    inv = ROPE_BASE ** (-jnp.arange(ROPE_HALF, dtype=F32) / ROPE_HALF)
    out = jax.ShapeDtypeStruct((n, ROPE_HALF), F32)
    return pl.pallas_call(
        _rope_table_kernel,
        grid=(n // tm,),
        in_specs=[pl.BlockSpec((tm, 1), lambda i: (i, 0)),
                  pl.BlockSpec((1, ROPE_HALF), lambda i: (0, 0))],
        out_specs=[pl.BlockSpec((tm, ROPE_HALF), lambda i: (i, 0))] * 2,
        out_shape=[out, out],
        compiler_params=_params(("parallel",)),
        name="rope_tables",
    )(positions.reshape(n, 1), inv.reshape(1, ROPE_HALF))


def _ret_proj_kernel(h_ref, g_ref, w_ref, cos_ref, sin_ref, zeta_ref,
                     q_ref, k_ref, kz_ref, v_ref, gate_ref):
    xn = _rms(h_ref[...], g_ref[...]).astype(BF16)
    cos = cos_ref[...]
    sin = sin_ref[...]
    k_scale = RET_DK ** -0.5

    def rotated(col):
        t = _dot(xn, w_ref[:, col:col + RET_DK])
        t1, t2 = t[:, :ROPE_HALF], t[:, ROPE_HALF:]
        return t1 * cos - t2 * sin, t1 * sin + t2 * cos

    for hd in range(RET_HEADS):
        lo = hd * RET_DK
        mid = lo + ROPE_HALF
        hi = lo + RET_DK
        q1, q2 = rotated(lo)
        q_ref[:, lo:mid] = q1.astype(BF16)
        q_ref[:, mid:hi] = q2.astype(BF16)
        k1, k2 = rotated(RET_QK + lo)
        k1 = k1 * k_scale
        k2 = k2 * k_scale
        k_ref[:, lo:mid] = k1.astype(BF16)
        k_ref[:, mid:hi] = k2.astype(BF16)
        kz_ref[:, lo:mid] = (k1 * zeta_ref[:, lo:mid]).astype(BF16)
        kz_ref[:, mid:hi] = (k2 * zeta_ref[:, mid:hi]).astype(BF16)
    width = 256
    for c in range(RET_V // width):
        cols = slice(c * width, (c + 1) * width)
        v_ref[:, cols] = _dot(xn, w_ref[:, 2 * RET_QK + c * width:
                                        2 * RET_QK + (c + 1) * width]).astype(BF16)
        gate_ref[:, cols] = _dot(xn, w_ref[:, 2 * RET_QK + RET_V + c * width:
                                           2 * RET_QK + RET_V + (c + 1) * width])


def _ret_proj(h, gains, layer, w_in, widx, cos, sin, zeta_tab):
    n = h.shape[0]
    g2 = layer * N_NORMS + 2
    tm = TOK_TM
    tok = lambda width: pl.BlockSpec((tm, width), lambda i: (i, 0))
    return pl.pallas_call(
        _ret_proj_kernel,
        grid=(n // tm,),
        in_specs=[
            tok(D_MODEL),
            pl.BlockSpec((None, 1, D_MODEL), lambda i: (g2, 0, 0)),
            _resident((None, D_MODEL, RET_IN), lambda i: (widx, 0, 0)),
            tok(ROPE_HALF), tok(ROPE_HALF),
            _resident((tm, RET_QK), lambda i: (0, 0)),
        ],
        out_specs=[tok(RET_QK), tok(RET_QK), tok(RET_QK), tok(RET_V), tok(RET_V)],
        out_shape=[jax.ShapeDtypeStruct((n, RET_QK), BF16)] * 3
        + [jax.ShapeDtypeStruct((n, RET_V), BF16),
           jax.ShapeDtypeStruct((n, RET_V), F32)],
        compiler_params=_params(("parallel",)),
        name="ret_proj",
    )(h, gains, w_in, cos, sin, zeta_tab)


def _ret_core_kernel(chunk_decay, q_ref, k_ref, kz_ref, v_ref, gate_ref,
                     decay_ref, xi_ref, gain_ref, o_ref, state_ref):
    @pl.when(pl.program_id(1) == 0)
    def _():
        state_ref[...] = jnp.zeros_like(state_ref)

    contract_last = (((1,), (1,)), ((), ()))
    contract_first = (((0,), (0,)), ((), ()))
    for c in range(RET_STEP // RET_CHUNK):
        rows = slice(c * RET_CHUNK, (c + 1) * RET_CHUNK)
        for hd in range(RET_HEADS):
            qk_cols = slice(hd * RET_DK, (hd + 1) * RET_DK)
            v_cols = slice(hd * RET_DV, (hd + 1) * RET_DV)
            q = q_ref[rows, qk_cols]
            v = v_ref[rows, v_cols]
            scores = lax.dot_general(q, k_ref[rows, qk_cols], contract_last,
                                     preferred_element_type=F32) * decay_ref[hd]
            state = state_ref[hd]
            o = _dot(scores.astype(BF16), v) + _dot(q, state.astype(BF16)) * xi_ref[hd]
            state_ref[hd] = state * chunk_decay[hd] + lax.dot_general(
                kz_ref[rows, qk_cols], v, contract_first, preferred_element_type=F32)
            mu = jnp.mean(o, axis=-1, keepdims=True)
            d = o - mu
            var = jnp.mean(d * d, axis=-1, keepdims=True)
            normed = d * lax.rsqrt(var + GN_EPS) * gain_ref[:, v_cols]
            o_ref[rows, v_cols] = (_silu(gate_ref[rows, v_cols]) * normed).astype(BF16)


def _ret_core(q, k, kz, v, gate, decay, xi, gn_gain, chunk_decay, batch, seq):
    n = q.shape[0]
    steps = seq // RET_STEP
    tok = lambda width: pl.BlockSpec((RET_STEP, width), lambda b, t: (b * steps + t, 0))
    return pl.pallas_call(
        functools.partial(_ret_core_kernel, chunk_decay),
        grid=(batch, steps),
        in_specs=[
            tok(RET_QK), tok(RET_QK), tok(RET_QK), tok(RET_V), tok(RET_V),
            _resident((RET_HEADS, RET_CHUNK, RET_CHUNK), lambda b, t: (0, 0, 0)),
            _resident((RET_HEADS, RET_CHUNK, RET_DV), lambda b, t: (0, 0, 0)),
            pl.BlockSpec((1, RET_V), lambda b, t: (0, 0)),
        ],
        out_specs=tok(RET_V),
        out_shape=jax.ShapeDtypeStruct((n, RET_V), BF16),
        scratch_shapes=[pltpu.VMEM((RET_HEADS, RET_DK, RET_DV), F32)],
        compiler_params=_params(("parallel", "arbitrary")),
        name="ret_core",
    )(q, k, kz, v, gate, decay, xi, gn_gain)


def _retention_constants():
    heads = jnp.arange(RET_HEADS, dtype=F32)
    log_gamma = jnp.log1p(-jnp.exp2(-5.0 - heads))
    idx = jnp.arange(RET_CHUNK, dtype=F32)
    rel = idx[:, None] - idx[None, :]
    inner = jnp.where(rel[None] >= 0,
                      jnp.exp(jnp.maximum(rel, 0.0)[None] * log_gamma[:, None, None]), 0.0)
    xi = jnp.exp((idx + 1.0)[None, :] * log_gamma[:, None])
    zeta = jnp.exp((RET_CHUNK - 1.0 - idx)[None, :] * log_gamma[:, None])
    return inner, xi, zeta


def _chunk_decay():
    return tuple(float((1.0 - 2.0 ** (-5 - hd)) ** RET_CHUNK) for hd in range(RET_HEADS))


def _sb_proj_kernel(h_ref, g_ref, w_ref, o_ref):
    xn = _rms(h_ref[...], g_ref[...]).astype(BF16)
    width = 512
    for c in range(SB_IN // width):
        cols = slice(c * width, (c + 1) * width)
        o_ref[:, cols] = _dot(xn, w_ref[:, cols]).astype(BF16)


def _sb_proj(h, gains, layer, w_in, widx):
    n = h.shape[0]
    g2 = layer * N_NORMS + 2
    return pl.pallas_call(
        _sb_proj_kernel,
        grid=(n // TOK_TM,),
        in_specs=[
            pl.BlockSpec((TOK_TM, D_MODEL), lambda i: (i, 0)),
            pl.BlockSpec((None, 1, D_MODEL), lambda i: (g2, 0, 0)),
            _resident((None, D_MODEL, SB_IN), lambda i: (widx, 0, 0)),
        ],
        out_specs=pl.BlockSpec((TOK_TM, SB_IN), lambda i: (i, 0)),
        out_shape=jax.ShapeDtypeStruct((n, SB_IN), BF16),
        compiler_params=_params(("parallel",)),
        name="sb_proj",
    )(h, gains, w_in)


def _sb_core_kernel(q_ref, k_ref, v_ref, o_ref, acc_ref, rest_ref):
    blk = SB_BLOCK
    seq = q_ref.shape[0]
    scale = SB_DH ** -0.5
    row = lax.broadcasted_iota(jnp.int32, (blk, blk), 0)
    col = lax.broadcasted_iota(jnp.int32, (blk, blk), 1)
    causal = col < row
    r2 = lax.broadcasted_iota(jnp.int32, (2 * blk, 2 * blk), 0) & (blk - 1)
    c2 = lax.broadcasted_iota(jnp.int32, (2 * blk, 2 * blk), 1)
    neg_suffix_ones = jnp.where((r2 >= c2) | (c2 >= blk), -1.0, 0.0).astype(BF16)
    contract_last = (((1,), (1,)), ((), ()))
    heads = range(SB_GROUP)
    cols = [slice(g * SB_DH, (g + 1) * SB_DH) for g in heads]

    def tiles(qblocks, nblk, diagonal):
        width = nblk * blk
        sub = [slice(c * blk, (c + 1) * blk) for c in range(nblk)]
        chains = [(slot, g, q0, k0) for slot, q0, k0 in qblocks for g in heads]
        count = len(chains)
        raws, zs, sums, worst = [None] * count, [None] * count, [None] * count, {}

        def scores(i):
            _, g, q0, k0 = chains[i]
            raws[i] = lax.dot_general(q_ref[pl.ds(q0, blk), cols[g]],
                                      k_ref[pl.ds(k0, width), cols[g]],
                                      contract_last, preferred_element_type=F32)

        def suffix_sums(i):
            zs[i] = raws[i] * scale
            decayed = jnp.exp2(jnp.abs(raws[i]) * (-scale * LOG2_E))
            softplus = jnp.maximum(zs[i], 0.0) + jnp.log(1.0 + decayed)
            pieces = [softplus[:, s] for s in sub]
            if diagonal:
                pieces[-1] = jnp.where(causal, pieces[-1], 0.0)
            split = []
            for piece in pieces:
                hi = piece.astype(BF16)
                split.append(jnp.concatenate(
                    [hi, (piece - hi.astype(F32)).astype(BF16)], axis=1))
            both = _dot(jnp.concatenate(split, axis=0), neg_suffix_ones)
            sums[i] = [both[s] for s in sub]

        def weigh(i):
            slot, g, _, k0 = chains[i]
            rest = None if diagonal else rest_ref[slot, g]
            weights = [None] * nblk
            for c in reversed(range(nblk)):
                logw = zs[i][:, sub[c]] + sums[i][c][:, :blk]
                if rest is not None:
                    logw = logw + rest
                a = jnp.exp(logw)
                if diagonal and c == nblk - 1:
                    a = jnp.where(causal, a, 0.0)
                weights[c] = a.astype(BF16)
                row_sum = sums[i][c][:, blk:]
                rest = row_sum if rest is None else rest + row_sum
            part = _dot(jnp.concatenate(weights, axis=1), v_ref[pl.ds(k0, width), cols[g]])
            if diagonal:
                acc_ref[slot, g] = part
            else:
                acc_ref[slot, g] += part
            rest_ref[slot, g] = rest
            worst[slot] = rest if slot not in worst else jnp.maximum(worst[slot], rest)

        for stage in (scores, suffix_sums, weigh):
            for i in range(count):
                stage(i)
        return [jnp.max(worst[slot]) for slot, _, _ in qblocks]

    def finish(slot, q0, j, worst):
        def more(c):
            j, worst = c
            return jnp.logical_and(j >= 0, worst > SB_SKIP_LOG)

        def step(c):
            j, _ = c
            return j - 1, tiles([(slot, q0, pl.multiple_of(j * blk, blk))], 1, False)[0]

        lax.while_loop(more, step, (j, worst))
        for g in heads:
            o_ref[pl.ds(q0, blk), cols[g]] = acc_ref[slot, g].astype(BF16)

    prefix = SB_FIRST_PASS - 1
    for qi in range(prefix):
        tiles([(0, qi * blk, 0)], qi + 1, True)
        for g in heads:
            o_ref[qi * blk:(qi + 1) * blk, cols[g]] = acc_ref[0, g].astype(BF16)

    def q_blocks(step, carry):
        first = prefix + step * SB_QBLOCKS
        qblocks = [(slot, pl.multiple_of((first + slot) * blk, blk),
                    pl.multiple_of((first + slot - prefix) * blk, blk))
                   for slot in range(SB_QBLOCKS)]
        worst = tiles(qblocks, SB_FIRST_PASS, True)
        for slot, q0, _ in qblocks:
            finish(slot, q0, first + slot - SB_FIRST_PASS, worst[slot])
        return carry

    lax.fori_loop(0, (seq // blk - prefix) // SB_QBLOCKS, q_blocks, 0)


def _sb_core(qkv, batch, seq):
    qkv = qkv.reshape(batch, seq, SB_IN)
    groups = SB_HEADS // SB_GROUP
    width = SB_GROUP * SB_DH
    heads = lambda part: pl.BlockSpec((None, seq, width),
                                      lambda b, g: (b, 0, part * groups + g))
    out = pl.pallas_call(
        _sb_core_kernel,
        grid=(batch, groups),
        in_specs=[heads(0), heads(1), heads(2)],
        out_specs=heads(0),
        out_shape=jax.ShapeDtypeStruct((batch, seq, SB_WIDTH), BF16),
        scratch_shapes=[pltpu.VMEM((SB_QBLOCKS, SB_GROUP, SB_BLOCK, SB_DH), F32),
                        pltpu.VMEM((SB_QBLOCKS, SB_GROUP, SB_BLOCK, SB_BLOCK), F32)],
        compiler_params=_params(("parallel", "parallel")),
        name="sb_core",
    )(qkv, qkv, qkv)
    return out.reshape(batch * seq, SB_WIDTH)


def kernel(x, p, positions, norm_gains, ffn_w_gate, ffn_w_up, ffn_w_down, ret_w_in,
           ret_gn_gain, ret_w_out, sb_w_in, sb_w_out, ple_w_gate, ple_w_proj):
    batch, seq, _ = x.shape
    n = batch * seq
    h = x.reshape(n, D_MODEL)
    p = p.reshape(DEPTH, n, PLE_DIM)
    gains = norm_gains.reshape(DEPTH * N_NORMS, 1, D_MODEL)
    wg, wu, wd = (w.astype(BF16) for w in (ffn_w_gate, ffn_w_up, ffn_w_down))
    ret_w_in, ret_w_out, sb_w_in, sb_w_out, ple_w_gate, ple_w_proj = (
        w.astype(BF16) for w in (ret_w_in, ret_w_out, sb_w_in, sb_w_out,
                                 ple_w_gate, ple_w_proj))

    cos, sin = _rope_tables(positions)
    inner, xi, zeta = _retention_constants()
    xi_tab = jnp.broadcast_to(xi[:, :, None], (RET_HEADS, RET_CHUNK, RET_DV))
    zeta_tab = jnp.tile(jnp.repeat(zeta.T, RET_DK, axis=1), (TOK_TM // RET_CHUNK, 1))
    chunk_decay = _chunk_decay()

    for layer in range(DEPTH):
        j = layer // N_MIXERS
        h = _ffn(h, gains, layer, 0, wg, wu, wd)
        if layer % N_MIXERS == 0:
            q, k, kz, v, gate = _ret_proj(h, gains, layer, ret_w_in, j, cos, sin, zeta_tab)
            mixed = _ret_core(q, k, kz, v, gate, inner, xi_tab,
                              ret_gn_gain[j].reshape(1, RET_V), chunk_decay, batch, seq)
            h = _outproj(h, mixed, gains, layer, ret_w_out, j)
        else:
            qkv = _sb_proj(h, gains, layer, sb_w_in, j)
            mixed = _sb_core(qkv, batch, seq)
            h = _outproj(h, mixed, gains, layer, sb_w_out, j)
        h = _ffn(h, gains, layer, 1, wg, wu, wd)
        h = _ple(h, p, gains, layer, ple_w_gate, ple_w_proj)
    return h.reshape(batch, seq, D_MODEL)
```

```python
import functools

import jax
import jax.numpy as jnp
from jax import lax
from jax.experimental import pallas as pl
from jax.experimental.pallas import tpu as pltpu

F32 = jnp.float32
BF16 = jnp.bfloat16

D_MODEL = 1024
DEPTH = 4
N_MIXERS = 2
PLE_DIM = 256
D_FF = 2816
FFN_RES_WEIGHT = 0.5
RET_HEADS = 4
RET_DK = D_MODEL // RET_HEADS
RET_QK = RET_HEADS * RET_DK
RET_DV = 2 * RET_DK
RET_V = RET_HEADS * RET_DV
RET_IN = 2 * RET_QK + 2 * RET_V
RET_CHUNK = 128
ROPE_BASE = 10000.0
ROPE_HALF = RET_DK // 2
GN_EPS = 1e-5
SB_HEADS = 8
SB_DH = D_MODEL // SB_HEADS
SB_WIDTH = SB_HEADS * SB_DH
SB_IN = 3 * SB_WIDTH
SB_BLOCK = 128
N_NORMS = 8
RMS_EPS = 1e-6
LOG2_E = 1.4426950408889634

VMEM_LIMIT_BYTES = 56 * 1024 * 1024

FFN_TM = 512
FFN_TF = 256
SB_GROUP = 4
SB_FIRST_PASS = 3
SB_QBLOCKS = 1
TOK_TM = 512
RET_STEP = 512

SB_SKIP_LOG = -105.0


def _params(sem):
    return pltpu.CompilerParams(dimension_semantics=sem,
                                vmem_limit_bytes=VMEM_LIMIT_BYTES)


def _resident(shape, index_map):
    return pl.BlockSpec(shape, index_map, pipeline_mode=pl.Buffered(1))


def _rms(x, g):
    ms = jnp.mean(x * x, axis=-1, keepdims=True)
    return x * lax.rsqrt(ms + RMS_EPS) * g


def _dot(a, b):
    return jnp.dot(a, b, preferred_element_type=F32)


def _silu(x):
    return x * jax.nn.sigmoid(x)


def _gain(gains_ref, k):
    return gains_ref[k:k + 1, :]


def _half_ffn(h, g_pre, g_post, wg_ref, wu_ref, wd_ref, xn_ref, hid_ref):
    xn_ref[...] = _rms(h, g_pre).astype(BF16)
    for c in range(D_FF // FFN_TF):
        cols = slice(c * FFN_TF, (c + 1) * FFN_TF)
        xn = xn_ref[...]
        hid_ref[:, cols] = (_silu(_dot(xn, wg_ref[:, cols]))
                            * _dot(xn, wu_ref[:, cols])).astype(BF16)
    f = _dot(hid_ref[...], wd_ref[...])
    return FFN_RES_WEIGHT * _rms(f, g_post)


class _CastJobs:
    def __init__(self, jobs, steps):
        self.operands, self.in_specs, self.out_specs, self.out_shapes, self.shapes = [], [], [], [], []
        for w, lead in jobs:
            rows, cols = w.shape[-2:]
            slab = rows // steps
            assert slab * steps == rows and len(lead) == w.ndim - 2
            self.operands.append(w.reshape(w.shape[:-2] + (steps, slab, cols)))
            self.in_specs.append(pl.BlockSpec(
                (None,) * (len(lead) + 1) + (slab, cols),
                functools.partial(lambda lead, i: lead + (i, 0, 0), tuple(lead))))
            self.out_specs.append(pl.BlockSpec((None, slab, cols), lambda i: (i, 0, 0)))
            self.out_shapes.append(jax.ShapeDtypeStruct((steps, slab, cols), BF16))
            self.shapes.append((rows, cols))

    def __len__(self):
        return len(self.operands)

    def matrices(self, outputs):
        return [o.reshape(shape) for o, shape in zip(outputs, self.shapes)]


def _cast_slabs(src_refs, dst_refs):
    for src, dst in zip(src_refs, dst_refs):
        dst[...] = src[...].astype(BF16)


def _whole(w):
    return _resident(w.shape, lambda i: (0, 0))


def _ffn_kernel(n_cast, h_ref, gains_ref, wg_ref, wu_ref, wd_ref, *refs):
    cast_src, (o_ref, *cast_dst), (xn_ref, hid_ref) = (
        refs[:n_cast], refs[n_cast:2 * n_cast + 1], refs[2 * n_cast + 1:])
    step = _half_ffn(h_ref[...], _gain(gains_ref, 0), _gain(gains_ref, 1),
                     wg_ref, wu_ref, wd_ref, xn_ref, hid_ref)
    o_ref[...] = h_ref[...] + step
    _cast_slabs(cast_src, cast_dst)


def _ffn_scratch():
    return [pltpu.VMEM((FFN_TM, D_MODEL), BF16), pltpu.VMEM((FFN_TM, D_FF), BF16)]


def _ffn(h, gains, layer, ffn_weights, cast_jobs):
    n = h.shape[0]
    steps = n // FFN_TM
    casts = _CastJobs(cast_jobs, steps)
    outs = pl.pallas_call(
        functools.partial(_ffn_kernel, len(casts)),
        grid=(steps,),
        in_specs=[
            pl.BlockSpec((FFN_TM, D_MODEL), lambda i: (i, 0)),
            pl.BlockSpec((None, N_NORMS, D_MODEL), lambda i: (layer, 0, 0)),
        ] + [_whole(w) for w in ffn_weights] + casts.in_specs,
        out_specs=[pl.BlockSpec((FFN_TM, D_MODEL), lambda i: (i, 0))] + casts.out_specs,
        out_shape=[jax.ShapeDtypeStruct(h.shape, F32)] + casts.out_shapes,
        scratch_shapes=_ffn_scratch(),
        compiler_params=_params(("parallel",)),
        name="ffn",
    )(h, gains, *ffn_weights, *casts.operands)
    return outs[0], casts.matrices(outs[1:])


def _tail_kernel(n_cast, h_ref, mixed_ref, p_ref, gains_ref, wout_ref, wg_ref, wu_ref,
                 wd_ref, wgate_ref, wproj_ref, *refs):
    cast_src, (o_ref, *cast_dst), (xn_ref, hid_ref, h_scratch) = (
        refs[:n_cast], refs[n_cast:2 * n_cast + 1], refs[2 * n_cast + 1:])
    gain = functools.partial(_gain, gains_ref)
    h_scratch[...] = h_ref[...] + _rms(_dot(mixed_ref[...], wout_ref[...]), gain(3))
    step = _half_ffn(h_scratch[...], gain(4), gain(5), wg_ref, wu_ref, wd_ref,
                     xn_ref, hid_ref)
    h_scratch[...] += step
    h = h_scratch[...]
    gate = jax.nn.sigmoid(_dot(_rms(h, gain(6)).astype(BF16), wgate_ref[...]))
    e = _dot(p_ref[...].astype(BF16), wproj_ref[...])
    o_ref[...] = h + _rms(gate * e, gain(7))
    _cast_slabs(cast_src, cast_dst)


def _tail(h, mixed, p, gains, layer, w_out, ffn_weights, ple_weights, cast_jobs):
    n = h.shape[0]
    steps = n // FFN_TM
    casts = _CastJobs(cast_jobs, steps)
    weights = [w_out, *ffn_weights, *ple_weights]
    outs = pl.pallas_call(
        functools.partial(_tail_kernel, len(casts)),
        grid=(steps,),
        in_specs=[
            pl.BlockSpec((FFN_TM, D_MODEL), lambda i: (i, 0)),
            pl.BlockSpec((FFN_TM, mixed.shape[1]), lambda i: (i, 0)),
            pl.BlockSpec((None, FFN_TM, PLE_DIM), lambda i: (layer, i, 0)),
            pl.BlockSpec((None, N_NORMS, D_MODEL), lambda i: (layer, 0, 0)),
        ] + [_whole(w) for w in weights] + casts.in_specs,
        out_specs=[pl.BlockSpec((FFN_TM, D_MODEL), lambda i: (i, 0))] + casts.out_specs,
        out_shape=[jax.ShapeDtypeStruct(h.shape, F32)] + casts.out_shapes,
        scratch_shapes=_ffn_scratch() + [pltpu.VMEM((FFN_TM, D_MODEL), F32)],
        compiler_params=_params(("parallel",)),
        name="tail",
    )(h, mixed, p, gains, *weights, *casts.operands)
    return outs[0], casts.matrices(outs[1:])


def _rope_table_kernel(pos_ref, inv_ref, cos_ref, sin_ref):
    ang = pos_ref[...].astype(F32) * inv_ref[...]
    cos_ref[...] = jnp.cos(ang)
    sin_ref[...] = jnp.sin(ang)


def _rope_tables(positions):
    n = positions.size
    tm = 1024
    inv = ROPE_BASE ** (-jnp.arange(ROPE_HALF, dtype=F32) / ROPE_HALF)
    out = jax.ShapeDtypeStruct((n, ROPE_HALF), F32)
    return pl.pallas_call(
        _rope_table_kernel,
        grid=(n // tm,),
        in_specs=[pl.BlockSpec((tm, 1), lambda i: (i, 0)),
                  pl.BlockSpec((1, ROPE_HALF), lambda i: (0, 0))],
        out_specs=[pl.BlockSpec((tm, ROPE_HALF), lambda i: (i, 0))] * 2,
        out_shape=[out, out],
        compiler_params=_params(("parallel",)),
        name="rope_tables",
    )(positions.reshape(n, 1), inv.reshape(1, ROPE_HALF))


def _ret_proj_kernel(h_ref, g_ref, w_ref, cos_ref, sin_ref, zeta_ref,
                     q_ref, k_ref, kz_ref, v_ref, gate_ref):
    xn = _rms(h_ref[...], _gain(g_ref, 2)).astype(BF16)
    cos = cos_ref[...]
    sin = sin_ref[...]
    k_scale = RET_DK ** -0.5

    def rotated(col):
        t = _dot(xn, w_ref[:, col:col + RET_DK])
        t1, t2 = t[:, :ROPE_HALF], t[:, ROPE_HALF:]
        return t1 * cos - t2 * sin, t1 * sin + t2 * cos

    for hd in range(RET_HEADS):
        lo = hd * RET_DK
        mid = lo + ROPE_HALF
        hi = lo + RET_DK
        q1, q2 = rotated(lo)
        q_ref[:, lo:mid] = q1.astype(BF16)
        q_ref[:, mid:hi] = q2.astype(BF16)
        k1, k2 = rotated(RET_QK + lo)
        k1 = k1 * k_scale
        k2 = k2 * k_scale
        k_ref[:, lo:mid] = k1.astype(BF16)
        k_ref[:, mid:hi] = k2.astype(BF16)
        kz_ref[:, lo:mid] = (k1 * zeta_ref[:, lo:mid]).astype(BF16)
        kz_ref[:, mid:hi] = (k2 * zeta_ref[:, mid:hi]).astype(BF16)
    width = 256
    for c in range(RET_V // width):
        cols = slice(c * width, (c + 1) * width)
        v_ref[:, cols] = _dot(xn, w_ref[:, 2 * RET_QK + c * width:
                                        2 * RET_QK + (c + 1) * width]).astype(BF16)
        gate_ref[:, cols] = _dot(xn, w_ref[:, 2 * RET_QK + RET_V + c * width:
                                           2 * RET_QK + RET_V + (c + 1) * width])


def _ret_proj(h, gains, layer, w_in, cos, sin, zeta_tab):
    n = h.shape[0]
    tm = TOK_TM
    tok = lambda width: pl.BlockSpec((tm, width), lambda i: (i, 0))
    return pl.pallas_call(
        _ret_proj_kernel,
        grid=(n // tm,),
        in_specs=[
            tok(D_MODEL),
            pl.BlockSpec((None, N_NORMS, D_MODEL), lambda i: (layer, 0, 0)),
            _whole(w_in),
            tok(ROPE_HALF), tok(ROPE_HALF),
            _resident((tm, RET_QK), lambda i: (0, 0)),
        ],
        out_specs=[tok(RET_QK), tok(RET_QK), tok(RET_QK), tok(RET_V), tok(RET_V)],
        out_shape=[jax.ShapeDtypeStruct((n, RET_QK), BF16)] * 3
        + [jax.ShapeDtypeStruct((n, RET_V), BF16),
           jax.ShapeDtypeStruct((n, RET_V), F32)],
        compiler_params=_params(("parallel",)),
        name="ret_proj",
    )(h, gains, w_in, cos, sin, zeta_tab)


def _ret_core_kernel(chunk_decay, q_ref, k_ref, kz_ref, v_ref, gate_ref,
                     decay_ref, xi_ref, gain_ref, o_ref, state_ref):
    @pl.when(pl.program_id(1) == 0)
    def _():
        state_ref[...] = jnp.zeros_like(state_ref)

    contract_last = (((1,), (1,)), ((), ()))
    contract_first = (((0,), (0,)), ((), ()))
    for c in range(RET_STEP // RET_CHUNK):
        rows = slice(c * RET_CHUNK, (c + 1) * RET_CHUNK)
        for hd in range(RET_HEADS):
            qk_cols = slice(hd * RET_DK, (hd + 1) * RET_DK)
            v_cols = slice(hd * RET_DV, (hd + 1) * RET_DV)
            q = q_ref[rows, qk_cols]
            v = v_ref[rows, v_cols]
            scores = lax.dot_general(q, k_ref[rows, qk_cols], contract_last,
                                     preferred_element_type=F32) * decay_ref[hd]
            state = state_ref[hd]
            o = _dot(scores.astype(BF16), v) + _dot(q, state.astype(BF16)) * xi_ref[hd]
            state_ref[hd] = state * chunk_decay[hd] + lax.dot_general(
                kz_ref[rows, qk_cols], v, contract_first, preferred_element_type=F32)
            mu = jnp.mean(o, axis=-1, keepdims=True)
            d = o - mu
            var = jnp.mean(d * d, axis=-1, keepdims=True)
            normed = d * lax.rsqrt(var + GN_EPS) * gain_ref[:, v_cols]
            o_ref[rows, v_cols] = (_silu(gate_ref[rows, v_cols]) * normed).astype(BF16)


def _ret_core(q, k, kz, v, gate, decay, xi, gn_gain, chunk_decay, batch, seq):
    n = q.shape[0]
    steps = seq // RET_STEP
    tok = lambda width: pl.BlockSpec((RET_STEP, width), lambda b, t: (b * steps + t, 0))
    return pl.pallas_call(
        functools.partial(_ret_core_kernel, chunk_decay),
        grid=(batch, steps),
        in_specs=[
            tok(RET_QK), tok(RET_QK), tok(RET_QK), tok(RET_V), tok(RET_V),
            _resident((RET_HEADS, RET_CHUNK, RET_CHUNK), lambda b, t: (0, 0, 0)),
            _resident((RET_HEADS, RET_CHUNK, RET_DV), lambda b, t: (0, 0, 0)),
            pl.BlockSpec((1, RET_V), lambda b, t: (0, 0)),
        ],
        out_specs=tok(RET_V),
        out_shape=jax.ShapeDtypeStruct((n, RET_V), BF16),
        scratch_shapes=[pltpu.VMEM((RET_HEADS, RET_DK, RET_DV), F32)],
        compiler_params=_params(("parallel", "arbitrary")),
        name="ret_core",
    )(q, k, kz, v, gate, decay, xi, gn_gain)


def _retention_constants():
    heads = jnp.arange(RET_HEADS, dtype=F32)
    log_gamma = jnp.log1p(-jnp.exp2(-5.0 - heads))
    idx = jnp.arange(RET_CHUNK, dtype=F32)
    rel = idx[:, None] - idx[None, :]
    inner = jnp.where(rel[None] >= 0,
                      jnp.exp(jnp.maximum(rel, 0.0)[None] * log_gamma[:, None, None]), 0.0)
    xi = jnp.exp((idx + 1.0)[None, :] * log_gamma[:, None])
    zeta = jnp.exp((RET_CHUNK - 1.0 - idx)[None, :] * log_gamma[:, None])
    return inner, xi, zeta


def _chunk_decay():
    return tuple(float((1.0 - 2.0 ** (-5 - hd)) ** RET_CHUNK) for hd in range(RET_HEADS))


def _sb_proj_kernel(h_ref, g_ref, w_ref, o_ref):
    xn = _rms(h_ref[...], _gain(g_ref, 2)).astype(BF16)
    width = 512
    for c in range(SB_IN // width):
        cols = slice(c * width, (c + 1) * width)
        o_ref[:, cols] = _dot(xn, w_ref[:, cols]).astype(BF16)


def _sb_proj(h, gains, layer, w_in):
    n = h.shape[0]
    return pl.pallas_call(
        _sb_proj_kernel,
        grid=(n // TOK_TM,),
        in_specs=[
            pl.BlockSpec((TOK_TM, D_MODEL), lambda i: (i, 0)),
            pl.BlockSpec((None, N_NORMS, D_MODEL), lambda i: (layer, 0, 0)),
            _whole(w_in),
        ],
        out_specs=pl.BlockSpec((TOK_TM, SB_IN), lambda i: (i, 0)),
        out_shape=jax.ShapeDtypeStruct((n, SB_IN), BF16),
        compiler_params=_params(("parallel",)),
        name="sb_proj",
    )(h, gains, w_in)


def _sb_core_kernel(q_ref, k_ref, v_ref, o_ref, acc_ref, rest_ref):
    blk = SB_BLOCK
    seq = q_ref.shape[0]
    scale = SB_DH ** -0.5
    row = lax.broadcasted_iota(jnp.int32, (blk, blk), 0)
    col = lax.broadcasted_iota(jnp.int32, (blk, blk), 1)
    causal = col < row
    r2 = lax.broadcasted_iota(jnp.int32, (2 * blk, 2 * blk), 0) & (blk - 1)
    c2 = lax.broadcasted_iota(jnp.int32, (2 * blk, 2 * blk), 1)
    neg_suffix_ones = jnp.where((r2 >= c2) | (c2 >= blk), -1.0, 0.0).astype(BF16)
    contract_last = (((1,), (1,)), ((), ()))
    heads = range(SB_GROUP)
    cols = [slice(g * SB_DH, (g + 1) * SB_DH) for g in heads]

    def tiles(qblocks, nblk, diagonal):
        width = nblk * blk
        sub = [slice(c * blk, (c + 1) * blk) for c in range(nblk)]
        chains = [(slot, g, q0, k0) for slot, q0, k0 in qblocks for g in heads]
        count = len(chains)
        raws, zs, sums, worst = [None] * count, [None] * count, [None] * count, {}

        def scores(i):
            _, g, q0, k0 = chains[i]
            raws[i] = lax.dot_general(q_ref[pl.ds(q0, blk), cols[g]],
                                      k_ref[pl.ds(k0, width), cols[g]],
                                      contract_last, preferred_element_type=F32)

        def suffix_sums(i):
            zs[i] = raws[i] * scale
            decayed = jnp.exp2(jnp.abs(raws[i]) * (-scale * LOG2_E))
            softplus = jnp.maximum(zs[i], 0.0) + jnp.log(1.0 + decayed)
            pieces = [softplus[:, s] for s in sub]
            if diagonal:
                pieces[-1] = jnp.where(causal, pieces[-1], 0.0)
            split = []
            for piece in pieces:
                hi = piece.astype(BF16)
                split.append(jnp.concatenate(
                    [hi, (piece - hi.astype(F32)).astype(BF16)], axis=1))
            both = _dot(jnp.concatenate(split, axis=0), neg_suffix_ones)
            sums[i] = [both[s] for s in sub]

        def weigh(i):
            slot, g, _, k0 = chains[i]
            rest = None if diagonal else rest_ref[slot, g]
            weights = [None] * nblk
            for c in reversed(range(nblk)):
                logw = zs[i][:, sub[c]] + sums[i][c][:, :blk]
                if rest is not None:
                    logw = logw + rest
                a = jnp.exp(logw)
                if diagonal and c == nblk - 1:
                    a = jnp.where(causal, a, 0.0)
                weights[c] = a.astype(BF16)
                row_sum = sums[i][c][:, blk:]
                rest = row_sum if rest is None else rest + row_sum
            part = _dot(jnp.concatenate(weights, axis=1), v_ref[pl.ds(k0, width), cols[g]])
            if diagonal:
                acc_ref[slot, g] = part
            else:
                acc_ref[slot, g] += part
            rest_ref[slot, g] = rest
            worst[slot] = rest if slot not in worst else jnp.maximum(worst[slot], rest)

        for stage in (scores, suffix_sums, weigh):
            for i in range(count):
                stage(i)
        return [jnp.max(worst[slot]) for slot, _, _ in qblocks]

    def finish(slot, q0, j, worst):
        def more(c):
            j, worst = c
            return jnp.logical_and(j >= 0, worst > SB_SKIP_LOG)

        def step(c):
            j, _ = c
            return j - 1, tiles([(slot, q0, pl.multiple_of(j * blk, blk))], 1, False)[0]

        lax.while_loop(more, step, (j, worst))
        for g in heads:
            o_ref[pl.ds(q0, blk), cols[g]] = acc_ref[slot, g].astype(BF16)

    prefix = SB_FIRST_PASS - 1
    for qi in range(prefix):
        tiles([(0, qi * blk, 0)], qi + 1, True)
        for g in heads:
            o_ref[qi * blk:(qi + 1) * blk, cols[g]] = acc_ref[0, g].astype(BF16)

    def q_blocks(step, carry):
        first = prefix + step * SB_QBLOCKS
        qblocks = [(slot, pl.multiple_of((first + slot) * blk, blk),
                    pl.multiple_of((first + slot - prefix) * blk, blk))
                   for slot in range(SB_QBLOCKS)]
        worst = tiles(qblocks, SB_FIRST_PASS, True)
        for slot, q0, _ in qblocks:
            finish(slot, q0, first + slot - SB_FIRST_PASS, worst[slot])
        return carry

    lax.fori_loop(0, (seq // blk - prefix) // SB_QBLOCKS, q_blocks, 0)


def _sb_core(qkv, batch, seq):
    qkv = qkv.reshape(batch, seq, SB_IN)
    groups = SB_HEADS // SB_GROUP
    width = SB_GROUP * SB_DH
    heads = lambda part: pl.BlockSpec((None, seq, width),
                                      lambda b, g: (b, 0, part * groups + g))
    out = pl.pallas_call(
        _sb_core_kernel,
        grid=(batch, groups),
        in_specs=[heads(0), heads(1), heads(2)],
        out_specs=heads(0),
        out_shape=jax.ShapeDtypeStruct((batch, seq, SB_WIDTH), BF16),
        scratch_shapes=[pltpu.VMEM((SB_QBLOCKS, SB_GROUP, SB_BLOCK, SB_DH), F32),
                        pltpu.VMEM((SB_QBLOCKS, SB_GROUP, SB_BLOCK, SB_BLOCK), F32)],
        compiler_params=_params(("parallel", "parallel")),
        name="sb_core",
    )(qkv, qkv, qkv)
    return out.reshape(batch * seq, SB_WIDTH)


def kernel(x, p, positions, norm_gains, ffn_w_gate, ffn_w_up, ffn_w_down, ret_w_in,
           ret_gn_gain, ret_w_out, sb_w_in, sb_w_out, ple_w_gate, ple_w_proj):
    batch, seq, _ = x.shape
    n = batch * seq
    h = x.reshape(n, D_MODEL)
    p = p.reshape(DEPTH, n, PLE_DIM)
    gains = norm_gains
    ffn_stacks = (ffn_w_gate, ffn_w_up, ffn_w_down)
    ffn_weights = [w[0, 0].astype(BF16) for w in ffn_stacks]

    cos, sin = _rope_tables(positions)
    inner, xi, zeta = _retention_constants()
    xi_tab = jnp.broadcast_to(xi[:, :, None], (RET_HEADS, RET_CHUNK, RET_DV))
    zeta_tab = jnp.tile(jnp.repeat(zeta.T, RET_DK, axis=1), (TOK_TM // RET_CHUNK, 1))
    chunk_decay = _chunk_decay()

    for layer in range(DEPTH):
        j = layer // N_MIXERS
        retention = layer % N_MIXERS == 0
        mixer_stacks = (ret_w_in, ret_w_out) if retention else (sb_w_in, sb_w_out)
        h, cast = _ffn(h, gains, layer, ffn_weights,
                       [(w, (j,)) for w in mixer_stacks]
                       + [(w, (layer, 1)) for w in ffn_stacks]
                       + [(ple_w_gate, (layer,)), (ple_w_proj, (layer,))])
        w_in, w_out, ffn_weights, ple_weights = cast[0], cast[1], cast[2:5], cast[5:7]
        if retention:
            q, k, kz, v, gate = _ret_proj(h, gains, layer, w_in, cos, sin, zeta_tab)
            mixed = _ret_core(q, k, kz, v, gate, inner, xi_tab,
                              ret_gn_gain[j].reshape(1, RET_V), chunk_decay, batch, seq)
        else:
            mixed = _sb_core(_sb_proj(h, gains, layer, w_in), batch, seq)
        next_ffn = [(w, (layer + 1, 0)) for w in ffn_stacks] if layer + 1 < DEPTH else []
        h, ffn_weights = _tail(h, mixed, p, gains, layer, w_out, ffn_weights, ple_weights,
                               next_ffn)
    return h.reshape(batch, seq, D_MODEL)
```

```python
import functools

import jax
import jax.numpy as jnp
from jax import lax
from jax.experimental import pallas as pl
from jax.experimental.pallas import tpu as pltpu

F32 = jnp.float32
BF16 = jnp.bfloat16

D_MODEL = 1024
DEPTH = 4
N_MIXERS = 2
PLE_DIM = 256
D_FF = 2816
FFN_RES_WEIGHT = 0.5
RET_HEADS = 4
RET_DK = D_MODEL // RET_HEADS
RET_QK = RET_HEADS * RET_DK
RET_DV = 2 * RET_DK
RET_V = RET_HEADS * RET_DV
RET_IN = 2 * RET_QK + 2 * RET_V
RET_CHUNK = 128
ROPE_BASE = 10000.0
ROPE_HALF = RET_DK // 2
GN_EPS = 1e-5
SB_HEADS = 8
SB_DH = D_MODEL // SB_HEADS
SB_WIDTH = SB_HEADS * SB_DH
SB_IN = 3 * SB_WIDTH
SB_BLOCK = 128
N_NORMS = 8
RMS_EPS = 1e-6
LOG2_E = 1.4426950408889634

VMEM_LIMIT_BYTES = 56 * 1024 * 1024

MXU_WIDTH = 256
FFN_TM = 512
FFN_TF = MXU_WIDTH
FFN_PARTS = 2
SB_GROUP = 4
SB_FIRST_PASS = 3
SB_QBLOCKS = 1
TOK_TM = 512
RET_STEP = 512

SB_SKIP_LOG = -105.0


def _params(sem):
    return pltpu.CompilerParams(dimension_semantics=sem,
                                vmem_limit_bytes=VMEM_LIMIT_BYTES)


def _resident(shape, index_map):
    return pl.BlockSpec(shape, index_map, pipeline_mode=pl.Buffered(1))


def _rms(x, g):
    ms = jnp.mean(x * x, axis=-1, keepdims=True)
    return x * lax.rsqrt(ms + RMS_EPS) * g


def _dot(a, b):
    return jnp.dot(a, b, preferred_element_type=F32)


def _silu(x):
    return x * jax.nn.sigmoid(x)


def _gain(gains_ref, k):
    return gains_ref[k:k + 1, :]


def _row_parts(rows):
    size = rows // FFN_PARTS
    return [slice(i * size, (i + 1) * size) for i in range(FFN_PARTS)]


def _half_ffn(parts, read_h, add_step, g_pre, g_post, wg_ref, wu_ref, wd_ref,
              xn_ref, hid_ref, side_work):
    def hidden(part, c):
        cols = slice(c * FFN_TF, (c + 1) * FFN_TF)
        xn = xn_ref[part, :]
        hid_ref[part, cols] = (_silu(_dot(xn, wg_ref[:, cols]))
                               * _dot(xn, wu_ref[:, cols])).astype(BF16)

    for i, part in enumerate(parts):
        xn_ref[part, :] = _rms(read_h(part), g_pre).astype(BF16)
        for c in range(D_FF // FFN_TF):
            hidden(part, c)
        if i == 0:
            side_work()
    g_step = g_post * FFN_RES_WEIGHT
    fs = [_dot(hid_ref[part, :], wd_ref[...]) for part in parts]
    for part, f in zip(parts, fs):
        add_step(part, _rms(f, g_step))


class _CastJobs:
    def __init__(self, jobs, steps):
        self.operands, self.in_specs, self.out_specs, self.out_shapes, self.shapes = [], [], [], [], []
        for w, lead in jobs:
            rows, cols = w.shape[-2:]
            slab = rows // steps
            assert slab * steps == rows and len(lead) == w.ndim - 2
            self.operands.append(w.reshape(w.shape[:-2] + (steps, slab, cols)))
            self.in_specs.append(pl.BlockSpec(
                (None,) * (len(lead) + 1) + (slab, cols),
                functools.partial(lambda lead, i: lead + (i, 0, 0), tuple(lead))))
            self.out_specs.append(pl.BlockSpec((None, slab, cols), lambda i: (i, 0, 0)))
            self.out_shapes.append(jax.ShapeDtypeStruct((steps, slab, cols), BF16))
            self.shapes.append((rows, cols))

    def __len__(self):
        return len(self.operands)

    def matrices(self, outputs):
        return [o.reshape(shape) for o, shape in zip(outputs, self.shapes)]


def _cast_slabs(src_refs, dst_refs):
    for src, dst in zip(src_refs, dst_refs):
        dst[...] = src[...].astype(BF16)


def _whole(w):
    return _resident(w.shape, lambda i: (0, 0))


def _ffn_kernel(n_cast, h_ref, gains_ref, wg_ref, wu_ref, wd_ref, *refs):
    cast_src, (o_ref, *cast_dst), (xn_ref, hid_ref) = (
        refs[:n_cast], refs[n_cast:2 * n_cast + 1], refs[2 * n_cast + 1:])
    def add_step(part, step):
        o_ref[part, :] = h_ref[part, :] + step

    _half_ffn(_row_parts(h_ref.shape[0]), lambda part: h_ref[part, :], add_step,
              _gain(gains_ref, 0), _gain(gains_ref, 1), wg_ref, wu_ref, wd_ref,
              xn_ref, hid_ref, functools.partial(_cast_slabs, cast_src, cast_dst))


def _ffn_scratch():
    return [pltpu.VMEM((FFN_TM, D_MODEL), BF16), pltpu.VMEM((FFN_TM, D_FF), BF16)]


def _ffn(h, gains, layer, ffn_weights, cast_jobs):
    n = h.shape[0]
    steps = n // FFN_TM
    casts = _CastJobs(cast_jobs, steps)
    outs = pl.pallas_call(
        functools.partial(_ffn_kernel, len(casts)),
        grid=(steps,),
        in_specs=[
            pl.BlockSpec((FFN_TM, D_MODEL), lambda i: (i, 0)),
            pl.BlockSpec((None, N_NORMS, D_MODEL), lambda i: (layer, 0, 0)),
        ] + [_whole(w) for w in ffn_weights] + casts.in_specs,
        out_specs=[pl.BlockSpec((FFN_TM, D_MODEL), lambda i: (i, 0))] + casts.out_specs,
        out_shape=[jax.ShapeDtypeStruct(h.shape, F32)] + casts.out_shapes,
        scratch_shapes=_ffn_scratch(),
        compiler_params=_params(("parallel",)),
        name="ffn",
    )(h, gains, *ffn_weights, *casts.operands)
    return outs[0], casts.matrices(outs[1:])


def _tail_kernel(n_cast, h_ref, mixed_ref, p_ref, gains_ref, wout_ref, wg_ref, wu_ref,
                 wd_ref, wgate_ref, wproj_ref, *refs):
    cast_src, (o_ref, *cast_dst), (xn_ref, hid_ref, h_scratch) = (
        refs[:n_cast], refs[n_cast:2 * n_cast + 1], refs[2 * n_cast + 1:])
    gain = functools.partial(_gain, gains_ref)
    parts = _row_parts(h_ref.shape[0])
    mixer_out = {part.start: _dot(mixed_ref[part, :], wout_ref[...]) for part in parts}

    def after_mixer(part):
        h = h_ref[part, :] + _rms(mixer_out[part.start], gain(3))
        h_scratch[part, :] = h
        return h

    def add_step_and_embedding(part, step):
        h = h_scratch[part, :] + step
        gate = jax.nn.sigmoid(_dot(_rms(h, gain(6)).astype(BF16), wgate_ref[...]))
        e = _dot(p_ref[part, :].astype(BF16), wproj_ref[...])
        o_ref[part, :] = h + _rms(gate * e, gain(7))

    _half_ffn(parts, after_mixer, add_step_and_embedding, gain(4), gain(5),
              wg_ref, wu_ref, wd_ref, xn_ref, hid_ref,
              functools.partial(_cast_slabs, cast_src, cast_dst))


def _tail(h, mixed, p, gains, layer, w_out, ffn_weights, ple_weights, cast_jobs):
    n = h.shape[0]
    steps = n // FFN_TM
    casts = _CastJobs(cast_jobs, steps)
    weights = [w_out, *ffn_weights, *ple_weights]
    outs = pl.pallas_call(
        functools.partial(_tail_kernel, len(casts)),
        grid=(steps,),
        in_specs=[
            pl.BlockSpec((FFN_TM, D_MODEL), lambda i: (i, 0)),
            pl.BlockSpec((FFN_TM, mixed.shape[1]), lambda i: (i, 0)),
            pl.BlockSpec((None, FFN_TM, PLE_DIM), lambda i: (layer, i, 0)),
            pl.BlockSpec((None, N_NORMS, D_MODEL), lambda i: (layer, 0, 0)),
        ] + [_whole(w) for w in weights] + casts.in_specs,
        out_specs=[pl.BlockSpec((FFN_TM, D_MODEL), lambda i: (i, 0))] + casts.out_specs,
        out_shape=[jax.ShapeDtypeStruct(h.shape, F32)] + casts.out_shapes,
        scratch_shapes=_ffn_scratch() + [pltpu.VMEM((FFN_TM, D_MODEL), F32)],
        compiler_params=_params(("parallel",)),
        name="tail",
    )(h, mixed, p, gains, *weights, *casts.operands)
    return outs[0], casts.matrices(outs[1:])


def _rope_table_kernel(pos_ref, inv_ref, cos_ref, sin_ref):
    ang = pos_ref[...].astype(F32) * inv_ref[...]
    cos_ref[...] = jnp.cos(ang)
    sin_ref[...] = jnp.sin(ang)


def _rope_tables(positions):
    n = positions.size
    tm = 1024
    inv = ROPE_BASE ** (-jnp.arange(ROPE_HALF, dtype=F32) / ROPE_HALF)
    out = jax.ShapeDtypeStruct((n, ROPE_HALF), F32)
    return pl.pallas_call(
        _rope_table_kernel,
        grid=(n // tm,),
        in_specs=[pl.BlockSpec((tm, 1), lambda i: (i, 0)),
                  pl.BlockSpec((1, ROPE_HALF), lambda i: (0, 0))],
        out_specs=[pl.BlockSpec((tm, ROPE_HALF), lambda i: (i, 0))] * 2,
        out_shape=[out, out],
        compiler_params=_params(("parallel",)),
        name="rope_tables",
    )(positions.reshape(n, 1), inv.reshape(1, ROPE_HALF))


def _ret_proj_kernel(h_ref, g_ref, w_ref, cos_ref, sin_ref, zeta_ref,
                     q_ref, k_ref, kz_ref, v_ref, gate_ref):
    k_scale = RET_DK ** -0.5
    for part in _row_parts(h_ref.shape[0]):
        xn = _rms(h_ref[part, :], _gain(g_ref, 2)).astype(BF16)
        cos = cos_ref[part, :]
        sin = sin_ref[part, :]

        def rotated(col):
            t = _dot(xn, w_ref[:, col:col + RET_DK])
            t1, t2 = t[:, :ROPE_HALF], t[:, ROPE_HALF:]
            return t1 * cos - t2 * sin, t1 * sin + t2 * cos

        for hd in range(RET_HEADS):
            lo = hd * RET_DK
            mid = lo + ROPE_HALF
            hi = lo + RET_DK
            q1, q2 = rotated(lo)
            q_ref[part, lo:mid] = q1.astype(BF16)
            q_ref[part, mid:hi] = q2.astype(BF16)
            k1, k2 = rotated(RET_QK + lo)
            k1 = k1 * k_scale
            k2 = k2 * k_scale
            k_ref[part, lo:mid] = k1.astype(BF16)
            k_ref[part, mid:hi] = k2.astype(BF16)
            kz_ref[part, lo:mid] = (k1 * zeta_ref[part, lo:mid]).astype(BF16)
            kz_ref[part, mid:hi] = (k2 * zeta_ref[part, mid:hi]).astype(BF16)
        for c in range(RET_V // MXU_WIDTH):
            cols = slice(c * MXU_WIDTH, (c + 1) * MXU_WIDTH)
            v0 = 2 * RET_QK + c * MXU_WIDTH
            g0 = v0 + RET_V
            v_ref[part, cols] = _dot(xn, w_ref[:, v0:v0 + MXU_WIDTH]).astype(BF16)
            gate_ref[part, cols] = _dot(xn, w_ref[:, g0:g0 + MXU_WIDTH])


def _ret_proj(h, gains, layer, w_in, cos, sin, zeta_tab):
    n = h.shape[0]
    tm = TOK_TM
    tok = lambda width: pl.BlockSpec((tm, width), lambda i: (i, 0))
    return pl.pallas_call(
        _ret_proj_kernel,
        grid=(n // tm,),
        in_specs=[
            tok(D_MODEL),
            pl.BlockSpec((None, N_NORMS, D_MODEL), lambda i: (layer, 0, 0)),
            _whole(w_in),
            tok(ROPE_HALF), tok(ROPE_HALF),
            _resident((tm, RET_QK), lambda i: (0, 0)),
        ],
        out_specs=[tok(RET_QK), tok(RET_QK), tok(RET_QK), tok(RET_V), tok(RET_V)],
        out_shape=[jax.ShapeDtypeStruct((n, RET_QK), BF16)] * 3
        + [jax.ShapeDtypeStruct((n, RET_V), BF16),
           jax.ShapeDtypeStruct((n, RET_V), F32)],
        compiler_params=_params(("parallel",)),
        name="ret_proj",
    )(h, gains, w_in, cos, sin, zeta_tab)


def _ret_core_kernel(chunk_decay, q_ref, k_ref, kz_ref, v_ref, gate_ref,
                     decay_ref, xi_ref, gain_ref, o_ref, state_ref):
    @pl.when(pl.program_id(1) == 0)
    def _():
        state_ref[...] = jnp.zeros_like(state_ref)

    contract_last = (((1,), (1,)), ((), ()))
    contract_first = (((0,), (0,)), ((), ()))
    for c in range(RET_STEP // RET_CHUNK):
        rows = slice(c * RET_CHUNK, (c + 1) * RET_CHUNK)
        for hd in range(RET_HEADS):
            qk_cols = slice(hd * RET_DK, (hd + 1) * RET_DK)
            v_cols = slice(hd * RET_DV, (hd + 1) * RET_DV)
            q = q_ref[rows, qk_cols]
            v = v_ref[rows, v_cols]
            scores = lax.dot_general(q, k_ref[rows, qk_cols], contract_last,
                                     preferred_element_type=F32) * decay_ref[hd]
            state = state_ref[hd]
            o = _dot(scores.astype(BF16), v) + _dot(q, state.astype(BF16)) * xi_ref[hd]
            state_ref[hd] = state * chunk_decay[hd] + lax.dot_general(
                kz_ref[rows, qk_cols], v, contract_first, preferred_element_type=F32)
            mu = jnp.mean(o, axis=-1, keepdims=True)
            d = o - mu
            var = jnp.mean(d * d, axis=-1, keepdims=True)
            normed = d * lax.rsqrt(var + GN_EPS) * gain_ref[:, v_cols]
            o_ref[rows, v_cols] = (_silu(gate_ref[rows, v_cols]) * normed).astype(BF16)


def _ret_core(q, k, kz, v, gate, decay, xi, gn_gain, chunk_decay, batch, seq):
    n = q.shape[0]
    steps = seq // RET_STEP
    tok = lambda width: pl.BlockSpec((RET_STEP, width), lambda b, t: (b * steps + t, 0))
    return pl.pallas_call(
        functools.partial(_ret_core_kernel, chunk_decay),
        grid=(batch, steps),
        in_specs=[
            tok(RET_QK), tok(RET_QK), tok(RET_QK), tok(RET_V), tok(RET_V),
            _resident((RET_HEADS, RET_CHUNK, RET_CHUNK), lambda b, t: (0, 0, 0)),
            _resident((RET_HEADS, RET_CHUNK, RET_DV), lambda b, t: (0, 0, 0)),
            pl.BlockSpec((1, RET_V), lambda b, t: (0, 0)),
        ],
        out_specs=tok(RET_V),
        out_shape=jax.ShapeDtypeStruct((n, RET_V), BF16),
        scratch_shapes=[pltpu.VMEM((RET_HEADS, RET_DK, RET_DV), F32)],
        compiler_params=_params(("parallel", "arbitrary")),
        name="ret_core",
    )(q, k, kz, v, gate, decay, xi, gn_gain)


def _retention_constants():
    heads = jnp.arange(RET_HEADS, dtype=F32)
    log_gamma = jnp.log1p(-jnp.exp2(-5.0 - heads))
    idx = jnp.arange(RET_CHUNK, dtype=F32)
    rel = idx[:, None] - idx[None, :]
    inner = jnp.where(rel[None] >= 0,
                      jnp.exp(jnp.maximum(rel, 0.0)[None] * log_gamma[:, None, None]), 0.0)
    xi = jnp.exp((idx + 1.0)[None, :] * log_gamma[:, None])
    zeta = jnp.exp((RET_CHUNK - 1.0 - idx)[None, :] * log_gamma[:, None])
    return inner, xi, zeta


def _chunk_decay():
    return tuple(float((1.0 - 2.0 ** (-5 - hd)) ** RET_CHUNK) for hd in range(RET_HEADS))


def _sb_proj_kernel(h_ref, g_ref, w_ref, o_ref):
    width = 2 * MXU_WIDTH
    for part in _row_parts(h_ref.shape[0]):
        xn = _rms(h_ref[part, :], _gain(g_ref, 2)).astype(BF16)
        for c in range(SB_IN // width):
            cols = slice(c * width, (c + 1) * width)
            o_ref[part, cols] = _dot(xn, w_ref[:, cols]).astype(BF16)


def _sb_proj(h, gains, layer, w_in):
    n = h.shape[0]
    return pl.pallas_call(
        _sb_proj_kernel,
        grid=(n // TOK_TM,),
        in_specs=[
            pl.BlockSpec((TOK_TM, D_MODEL), lambda i: (i, 0)),
            pl.BlockSpec((None, N_NORMS, D_MODEL), lambda i: (layer, 0, 0)),
            _whole(w_in),
        ],
        out_specs=pl.BlockSpec((TOK_TM, SB_IN), lambda i: (i, 0)),
        out_shape=jax.ShapeDtypeStruct((n, SB_IN), BF16),
        compiler_params=_params(("parallel",)),
        name="sb_proj",
    )(h, gains, w_in)


def _sb_core_kernel(q_ref, k_ref, v_ref, o_ref, acc_ref, rest_ref):
    blk = SB_BLOCK
    seq = q_ref.shape[0]
    scale = SB_DH ** -0.5
    row = lax.broadcasted_iota(jnp.int32, (blk, blk), 0)
    col = lax.broadcasted_iota(jnp.int32, (blk, blk), 1)
    causal = col < row
    r2 = lax.broadcasted_iota(jnp.int32, (2 * blk, 2 * blk), 0) & (blk - 1)
    c2 = lax.broadcasted_iota(jnp.int32, (2 * blk, 2 * blk), 1)
    neg_suffix_ones = jnp.where((r2 >= c2) | (c2 >= blk), -1.0, 0.0).astype(BF16)
    contract_last = (((1,), (1,)), ((), ()))
    heads = range(SB_GROUP)
    cols = [slice(g * SB_DH, (g + 1) * SB_DH) for g in heads]

    def tiles(qblocks, nblk, diagonal):
        width = nblk * blk
        sub = [slice(c * blk, (c + 1) * blk) for c in range(nblk)]
        chains = [(slot, g, q0, k0) for slot, q0, k0 in qblocks for g in heads]
        count = len(chains)
        raws, zs, sums, worst = [None] * count, [None] * count, [None] * count, {}

        def scores(i):
            _, g, q0, k0 = chains[i]
            raws[i] = lax.dot_general(q_ref[pl.ds(q0, blk), cols[g]],
                                      k_ref[pl.ds(k0, width), cols[g]],
                                      contract_last, preferred_element_type=F32)

        def suffix_sums(i):
            zs[i] = raws[i] * scale
            decayed = jnp.exp2(jnp.abs(raws[i]) * (-scale * LOG2_E))
            softplus = jnp.maximum(zs[i], 0.0) + jnp.log(1.0 + decayed)
            pieces = [softplus[:, s] for s in sub]
            if diagonal:
                pieces[-1] = jnp.where(causal, pieces[-1], 0.0)
            split = []
            for piece in pieces:
                hi = piece.astype(BF16)
                split.append(jnp.concatenate(
                    [hi, (piece - hi.astype(F32)).astype(BF16)], axis=1))
            both = _dot(jnp.concatenate(split, axis=0), neg_suffix_ones)
            sums[i] = [both[s] for s in sub]

        def weigh(i):
            slot, g, _, k0 = chains[i]
            rest = None if diagonal else rest_ref[slot, g]
            weights = [None] * nblk
            for c in reversed(range(nblk)):
                logw = zs[i][:, sub[c]] + sums[i][c][:, :blk]
                if rest is not None:
                    logw = logw + rest
                a = jnp.exp(logw)
                if diagonal and c == nblk - 1:
                    a = jnp.where(causal, a, 0.0)
                weights[c] = a.astype(BF16)
                row_sum = sums[i][c][:, blk:]
                rest = row_sum if rest is None else rest + row_sum
            part = _dot(jnp.concatenate(weights, axis=1), v_ref[pl.ds(k0, width), cols[g]])
            if diagonal:
                acc_ref[slot, g] = part
            else:
                acc_ref[slot, g] += part
            rest_ref[slot, g] = rest
            worst[slot] = rest if slot not in worst else jnp.maximum(worst[slot], rest)

        for stage in (scores, suffix_sums, weigh):
            for i in range(count):
                stage(i)
        return [jnp.max(worst[slot]) for slot, _, _ in qblocks]

    def finish(slot, q0, j, worst):
        def more(c):
            j, worst = c
            return jnp.logical_and(j >= 0, worst > SB_SKIP_LOG)

        def step(c):
            j, _ = c
            return j - 1, tiles([(slot, q0, pl.multiple_of(j * blk, blk))], 1, False)[0]

        lax.while_loop(more, step, (j, worst))
        for g in heads:
            o_ref[pl.ds(q0, blk), cols[g]] = acc_ref[slot, g].astype(BF16)

    prefix = SB_FIRST_PASS - 1
    for qi in range(prefix):
        tiles([(0, qi * blk, 0)], qi + 1, True)
        for g in heads:
            o_ref[qi * blk:(qi + 1) * blk, cols[g]] = acc_ref[0, g].astype(BF16)

    def q_blocks(step, carry):
        first = prefix + step * SB_QBLOCKS
        qblocks = [(slot, pl.multiple_of((first + slot) * blk, blk),
                    pl.multiple_of((first + slot - prefix) * blk, blk))
                   for slot in range(SB_QBLOCKS)]
        worst = tiles(qblocks, SB_FIRST_PASS, True)
        for slot, q0, _ in qblocks:
            finish(slot, q0, first + slot - SB_FIRST_PASS, worst[slot])
        return carry

    lax.fori_loop(0, (seq // blk - prefix) // SB_QBLOCKS, q_blocks, 0)


def _sb_core(qkv, batch, seq):
    qkv = qkv.reshape(batch, seq, SB_IN)
    groups = SB_HEADS // SB_GROUP
    width = SB_GROUP * SB_DH
    heads = lambda part: pl.BlockSpec((None, seq, width),
                                      lambda b, g: (b, 0, part * groups + g))
    out = pl.pallas_call(
        _sb_core_kernel,
        grid=(batch, groups),
        in_specs=[heads(0), heads(1), heads(2)],
        out_specs=heads(0),
        out_shape=jax.ShapeDtypeStruct((batch, seq, SB_WIDTH), BF16),
        scratch_shapes=[pltpu.VMEM((SB_QBLOCKS, SB_GROUP, SB_BLOCK, SB_DH), F32),
                        pltpu.VMEM((SB_QBLOCKS, SB_GROUP, SB_BLOCK, SB_BLOCK), F32)],
        compiler_params=_params(("parallel", "parallel")),
        name="sb_core",
    )(qkv, qkv, qkv)
    return out.reshape(batch * seq, SB_WIDTH)


def kernel(x, p, positions, norm_gains, ffn_w_gate, ffn_w_up, ffn_w_down, ret_w_in,
           ret_gn_gain, ret_w_out, sb_w_in, sb_w_out, ple_w_gate, ple_w_proj):
    batch, seq, _ = x.shape
    n = batch * seq
    h = x.reshape(n, D_MODEL)
    p = p.reshape(DEPTH, n, PLE_DIM)
    gains = norm_gains
    ffn_stacks = (ffn_w_gate, ffn_w_up, ffn_w_down)
    ffn_weights = [w[0, 0].astype(BF16) for w in ffn_stacks]

    cos, sin = _rope_tables(positions)
    inner, xi, zeta = _retention_constants()
    xi_tab = jnp.broadcast_to(xi[:, :, None], (RET_HEADS, RET_CHUNK, RET_DV))
    zeta_tab = jnp.tile(jnp.repeat(zeta.T, RET_DK, axis=1), (TOK_TM // RET_CHUNK, 1))
    chunk_decay = _chunk_decay()

    for layer in range(DEPTH):
        j = layer // N_MIXERS
        retention = layer % N_MIXERS == 0
        mixer_stacks = (ret_w_in, ret_w_out) if retention else (sb_w_in, sb_w_out)
        h, cast = _ffn(h, gains, layer, ffn_weights,
                       [(w, (j,)) for w in mixer_stacks]
                       + [(w, (layer, 1)) for w in ffn_stacks]
                       + [(ple_w_gate, (layer,)), (ple_w_proj, (layer,))])
        w_in, w_out, ffn_weights, ple_weights = cast[0], cast[1], cast[2:5], cast[5:7]
        if retention:
            q, k, kz, v, gate = _ret_proj(h, gains, layer, w_in, cos, sin, zeta_tab)
            mixed = _ret_core(q, k, kz, v, gate, inner, xi_tab,
                              ret_gn_gain[j].reshape(1, RET_V), chunk_decay, batch, seq)
        else:
            mixed = _sb_core(_sb_proj(h, gains, layer, w_in), batch, seq)
        next_ffn = [(w, (layer + 1, 0)) for w in ffn_stacks] if layer + 1 < DEPTH else []
        h, ffn_weights = _tail(h, mixed, p, gains, layer, w_out, ffn_weights, ple_weights,
                               next_ffn)
    return h.reshape(batch, seq, D_MODEL)
```

```python
import functools

import jax
import jax.numpy as jnp
from jax import lax
from jax.experimental import pallas as pl
from jax.experimental.pallas import tpu as pltpu

F32 = jnp.float32
BF16 = jnp.bfloat16

D_MODEL = 1024
DEPTH = 4
N_MIXERS = 2
PLE_DIM = 256
D_FF = 2816
FFN_RES_WEIGHT = 0.5
RET_HEADS = 4
RET_DK = D_MODEL // RET_HEADS
RET_QK = RET_HEADS * RET_DK
RET_DV = 2 * RET_DK
RET_V = RET_HEADS * RET_DV
RET_IN = 2 * RET_QK + 2 * RET_V
RET_CHUNK = 128
ROPE_BASE = 10000.0
ROPE_HALF = RET_DK // 2
GN_EPS = 1e-5
SB_HEADS = 8
SB_DH = D_MODEL // SB_HEADS
SB_WIDTH = SB_HEADS * SB_DH
SB_IN = 3 * SB_WIDTH
SB_BLOCK = 128
N_NORMS = 8
RMS_EPS = 1e-6
LOG2_E = 1.4426950408889634

VMEM_LIMIT_BYTES = 56 * 1024 * 1024

MXU_WIDTH = 256
FFN_TM = 512
FFN_TF = MXU_WIDTH
FFN_PARTS = 2
SB_GROUP = 4
SB_FIRST_PASS = 3
SB_QBLOCKS = 1
TOK_TM = 512
RET_STEP = 512

SB_SKIP_LOG = -105.0


def _params(sem):
    return pltpu.CompilerParams(dimension_semantics=sem,
                                vmem_limit_bytes=VMEM_LIMIT_BYTES)


def _resident(shape, index_map):
    return pl.BlockSpec(shape, index_map, pipeline_mode=pl.Buffered(1))


def _rms(x, g):
    ms = jnp.mean(x * x, axis=-1, keepdims=True)
    return x * lax.rsqrt(ms + RMS_EPS) * g


def _dot(a, b):
    return jnp.dot(a, b, preferred_element_type=F32)


def _silu(x):
    return x * jax.nn.sigmoid(x)


def _gain(gains_ref, k):
    return gains_ref[k:k + 1, :]


def _row_parts(rows):
    size = rows // FFN_PARTS
    return [slice(i * size, (i + 1) * size) for i in range(FFN_PARTS)]


def _half_ffn(parts, read_h, add_step, g_pre, g_post, wg_ref, wu_ref, wd_ref,
              xn_ref, hid_ref, side_work):
    def hidden(part, c):
        cols = slice(c * FFN_TF, (c + 1) * FFN_TF)
        xn = xn_ref[part, :]
        hid_ref[part, cols] = (_silu(_dot(xn, wg_ref[:, cols]))
                               * _dot(xn, wu_ref[:, cols])).astype(BF16)

    for i, part in enumerate(parts):
        xn_ref[part, :] = _rms(read_h(part), g_pre).astype(BF16)
        for c in range(D_FF // FFN_TF):
            hidden(part, c)
        if i == 0:
            side_work()
    g_step = g_post * FFN_RES_WEIGHT
    fs = [_dot(hid_ref[part, :], wd_ref[...]) for part in parts]
    for part, f in zip(parts, fs):
        add_step(part, _rms(f, g_step))


class _CastJobs:
    def __init__(self, jobs, steps):
        self.operands, self.in_specs, self.out_specs, self.out_shapes, self.shapes = [], [], [], [], []
        for w, lead in jobs:
            rows, cols = w.shape[-2:]
            slab = rows // steps
            assert slab * steps == rows and len(lead) == w.ndim - 2
            self.operands.append(w.reshape(w.shape[:-2] + (steps, slab, cols)))
            self.in_specs.append(pl.BlockSpec(
                (None,) * (len(lead) + 1) + (slab, cols),
                functools.partial(lambda lead, i: lead + (i, 0, 0), tuple(lead))))
            self.out_specs.append(pl.BlockSpec((None, slab, cols), lambda i: (i, 0, 0)))
            self.out_shapes.append(jax.ShapeDtypeStruct((steps, slab, cols), BF16))
            self.shapes.append((rows, cols))

    def __len__(self):
        return len(self.operands)

    def matrices(self, outputs):
        return [o.reshape(shape) for o, shape in zip(outputs, self.shapes)]


def _cast_slabs(src_refs, dst_refs):
    for src, dst in zip(src_refs, dst_refs):
        dst[...] = src[...].astype(BF16)


def _whole(w):
    return _resident(w.shape, lambda i: (0, 0))


def _ffn_kernel(n_cast, h_ref, gains_ref, wg_ref, wu_ref, wd_ref, *refs):
    cast_src, (o_ref, *cast_dst), (xn_ref, hid_ref) = (
        refs[:n_cast], refs[n_cast:2 * n_cast + 1], refs[2 * n_cast + 1:])
    def add_step(part, step):
        o_ref[part, :] = h_ref[part, :] + step

    _half_ffn(_row_parts(h_ref.shape[0]), lambda part: h_ref[part, :], add_step,
              _gain(gains_ref, 0), _gain(gains_ref, 1), wg_ref, wu_ref, wd_ref,
              xn_ref, hid_ref, functools.partial(_cast_slabs, cast_src, cast_dst))


def _ffn_scratch():
    return [pltpu.VMEM((FFN_TM, D_MODEL), BF16), pltpu.VMEM((FFN_TM, D_FF), BF16)]


def _ffn(h, gains, layer, ffn_weights, cast_jobs):
    n = h.shape[0]
    steps = n // FFN_TM
    casts = _CastJobs(cast_jobs, steps)
    outs = pl.pallas_call(
        functools.partial(_ffn_kernel, len(casts)),
        grid=(steps,),
        in_specs=[
            pl.BlockSpec((FFN_TM, D_MODEL), lambda i: (i, 0)),
            pl.BlockSpec((None, N_NORMS, D_MODEL), lambda i: (layer, 0, 0)),
        ] + [_whole(w) for w in ffn_weights] + casts.in_specs,
        out_specs=[pl.BlockSpec((FFN_TM, D_MODEL), lambda i: (i, 0))] + casts.out_specs,
        out_shape=[jax.ShapeDtypeStruct(h.shape, F32)] + casts.out_shapes,
        scratch_shapes=_ffn_scratch(),
        compiler_params=_params(("parallel",)),
        name="ffn",
    )(h, gains, *ffn_weights, *casts.operands)
    return outs[0], casts.matrices(outs[1:])


def _tail_kernel(n_cast, h_ref, mixed_ref, p_ref, gains_ref, wout_ref, wg_ref, wu_ref,
                 wd_ref, wgate_ref, wproj_ref, *refs):
    cast_src, (o_ref, *cast_dst), (xn_ref, hid_ref, h_scratch) = (
        refs[:n_cast], refs[n_cast:2 * n_cast + 1], refs[2 * n_cast + 1:])
    gain = functools.partial(_gain, gains_ref)
    parts = _row_parts(h_ref.shape[0])
    mixer_out = {part.start: _dot(mixed_ref[part, :], wout_ref[...]) for part in parts}

    def after_mixer(part):
        h = h_ref[part, :] + _rms(mixer_out[part.start], gain(3))
        h_scratch[part, :] = h
        return h

    def add_step_and_embedding(part, step):
        h = h_scratch[part, :] + step
        gate = jax.nn.sigmoid(_dot(_rms(h, gain(6)).astype(BF16), wgate_ref[...]))
        e = _dot(p_ref[part, :].astype(BF16), wproj_ref[...])
        o_ref[part, :] = h + _rms(gate * e, gain(7))

    _half_ffn(parts, after_mixer, add_step_and_embedding, gain(4), gain(5),
              wg_ref, wu_ref, wd_ref, xn_ref, hid_ref,
              functools.partial(_cast_slabs, cast_src, cast_dst))


def _tail(h, mixed, p, gains, layer, w_out, ffn_weights, ple_weights, cast_jobs):
    n = h.shape[0]
    steps = n // FFN_TM
    casts = _CastJobs(cast_jobs, steps)
    weights = [w_out, *ffn_weights, *ple_weights]
    outs = pl.pallas_call(
        functools.partial(_tail_kernel, len(casts)),
        grid=(steps,),
        in_specs=[
            pl.BlockSpec((FFN_TM, D_MODEL), lambda i: (i, 0)),
            pl.BlockSpec((FFN_TM, mixed.shape[1]), lambda i: (i, 0)),
            pl.BlockSpec((None, FFN_TM, PLE_DIM), lambda i: (layer, i, 0)),
            pl.BlockSpec((None, N_NORMS, D_MODEL), lambda i: (layer, 0, 0)),
        ] + [_whole(w) for w in weights] + casts.in_specs,
        out_specs=[pl.BlockSpec((FFN_TM, D_MODEL), lambda i: (i, 0))] + casts.out_specs,
        out_shape=[jax.ShapeDtypeStruct(h.shape, F32)] + casts.out_shapes,
        scratch_shapes=_ffn_scratch() + [pltpu.VMEM((FFN_TM, D_MODEL), F32)],
        compiler_params=_params(("parallel",)),
        name="tail",
    )(h, mixed, p, gains, *weights, *casts.operands)
    return outs[0], casts.matrices(outs[1:])


def _rope_table_kernel(pos_ref, inv_ref, cos_ref, sin_ref):
    ang = pos_ref[...].astype(F32) * inv_ref[...]
    cos_ref[...] = jnp.cos(ang)
    sin_ref[...] = jnp.sin(ang)


def _rope_tables(positions):
    n = positions.size
    tm = 1024
    inv = ROPE_BASE ** (-jnp.arange(ROPE_HALF, dtype=F32) / ROPE_HALF)
    out = jax.ShapeDtypeStruct((n, ROPE_HALF), F32)
    return pl.pallas_call(
        _rope_table_kernel,
        grid=(n // tm,),
        in_specs=[pl.BlockSpec((tm, 1), lambda i: (i, 0)),
                  pl.BlockSpec((1, ROPE_HALF), lambda i: (0, 0))],
        out_specs=[pl.BlockSpec((tm, ROPE_HALF), lambda i: (i, 0))] * 2,
        out_shape=[out, out],
        compiler_params=_params(("parallel",)),
        name="rope_tables",
    )(positions.reshape(n, 1), inv.reshape(1, ROPE_HALF))


def _retention_kernel(chunk_decay, h_ref, g_ref, w_ref, cos_ref, sin_ref, zeta_ref,
                      decay_ref, xi_ref, gn_gain_ref, o_ref,
                      state_ref, xn_ref, q_ref, k_ref, kz_ref, v_ref, gate_ref):
    @pl.when(pl.program_id(1) == 0)
    def _():
        state_ref[...] = jnp.zeros_like(state_ref)

    xn_ref[...] = _rms(h_ref[...], _gain(g_ref, 2)).astype(BF16)
    k_scale = RET_DK ** -0.5
    contract_last = (((1,), (1,)), ((), ()))
    contract_first = (((0,), (0,)), ((), ()))

    def projection_steps(hd):
        lo = hd * RET_DK
        mid = lo + ROPE_HALF
        hi = lo + RET_DK

        def rotated(col):
            t = _dot(xn_ref[...], w_ref[:, col:col + RET_DK])
            t1, t2 = t[:, :ROPE_HALF], t[:, ROPE_HALF:]
            cos = cos_ref[...]
            sin = sin_ref[...]
            return t1 * cos - t2 * sin, t1 * sin + t2 * cos

        def queries():
            q1, q2 = rotated(lo)
            q_ref[:, lo:mid] = q1.astype(BF16)
            q_ref[:, mid:hi] = q2.astype(BF16)

        def keys():
            k1, k2 = rotated(RET_QK + lo)
            k1 = k1 * k_scale
            k2 = k2 * k_scale
            k_ref[:, lo:mid] = k1.astype(BF16)
            k_ref[:, mid:hi] = k2.astype(BF16)
            kz_ref[:, lo:mid] = (k1 * zeta_ref[:, lo:mid]).astype(BF16)
            kz_ref[:, mid:hi] = (k2 * zeta_ref[:, mid:hi]).astype(BF16)

        def values(c):
            cols = slice(hd * RET_DV + c * MXU_WIDTH, hd * RET_DV + (c + 1) * MXU_WIDTH)
            v0 = 2 * RET_QK + cols.start
            v_ref[:, cols] = _dot(xn_ref[...], w_ref[:, v0:v0 + MXU_WIDTH]).astype(BF16)

        def gates(c):
            cols = slice(hd * RET_DV + c * MXU_WIDTH, hd * RET_DV + (c + 1) * MXU_WIDTH)
            g0 = 2 * RET_QK + RET_V + cols.start
            gate_ref[:, cols] = _dot(xn_ref[...], w_ref[:, g0:g0 + MXU_WIDTH])

        halves = range(RET_DV // MXU_WIDTH)
        return ([queries, keys] + [functools.partial(values, c) for c in halves]
                + [functools.partial(gates, c) for c in halves])

    def recurrence_steps(hd):
        return [functools.partial(recur, hd, c) for c in range(RET_STEP // RET_CHUNK)]

    def recur(hd, c):
        qk_cols = slice(hd * RET_DK, (hd + 1) * RET_DK)
        v_cols = slice(hd * RET_DV, (hd + 1) * RET_DV)
        rows = slice(c * RET_CHUNK, (c + 1) * RET_CHUNK)
        q = q_ref[rows, qk_cols]
        v = v_ref[rows, v_cols]
        scores = lax.dot_general(q, k_ref[rows, qk_cols], contract_last,
                                 preferred_element_type=F32) * decay_ref[hd]
        state = state_ref[hd]
        o = _dot(scores.astype(BF16), v) + _dot(q, state.astype(BF16)) * xi_ref[hd]
        state_ref[hd] = state * chunk_decay[hd] + lax.dot_general(
            kz_ref[rows, qk_cols], v, contract_first, preferred_element_type=F32)
        mu = jnp.mean(o, axis=-1, keepdims=True)
        d = o - mu
        var = jnp.mean(d * d, axis=-1, keepdims=True)
        normed = d * lax.rsqrt(var + GN_EPS) * gn_gain_ref[:, v_cols]
        o_ref[rows, v_cols] = (_silu(gate_ref[rows, v_cols]) * normed).astype(BF16)

    for step in projection_steps(0):
        step()
    for hd in range(RET_HEADS):
        ahead = projection_steps(hd + 1) if hd + 1 < RET_HEADS else []
        chunks = recurrence_steps(hd)
        for i in range(max(len(ahead), len(chunks))):
            for steps in (ahead, chunks):
                if i < len(steps):
                    steps[i]()


def _retention(h, gains, layer, w_in, cos, sin, zeta_tab, decay, xi, gn_gain, chunk_decay,
               batch, seq):
    n = h.shape[0]
    steps = seq // RET_STEP
    tok = lambda width: pl.BlockSpec((RET_STEP, width), lambda b, t: (b * steps + t, 0))
    staged = lambda width, dtype: pltpu.VMEM((RET_STEP, width), dtype)
    return pl.pallas_call(
        functools.partial(_retention_kernel, chunk_decay),
        grid=(batch, steps),
        in_specs=[
            tok(D_MODEL),
            pl.BlockSpec((None, N_NORMS, D_MODEL), lambda b, t: (layer, 0, 0)),
            _resident(w_in.shape, lambda b, t: (0, 0)),
            tok(ROPE_HALF), tok(ROPE_HALF),
            _resident((RET_STEP, RET_QK), lambda b, t: (0, 0)),
            _resident((RET_HEADS, RET_CHUNK, RET_CHUNK), lambda b, t: (0, 0, 0)),
            _resident((RET_HEADS, RET_CHUNK, RET_DV), lambda b, t: (0, 0, 0)),
            pl.BlockSpec((1, RET_V), lambda b, t: (0, 0)),
        ],
        out_specs=tok(RET_V),
        out_shape=jax.ShapeDtypeStruct((n, RET_V), BF16),
        scratch_shapes=[pltpu.VMEM((RET_HEADS, RET_DK, RET_DV), F32),
                        staged(D_MODEL, BF16), staged(RET_QK, BF16), staged(RET_QK, BF16),
                        staged(RET_QK, BF16), staged(RET_V, BF16), staged(RET_V, F32)],
        compiler_params=_params(("parallel", "arbitrary")),
        name="retention",
    )(h, gains, w_in, cos, sin, zeta_tab, decay, xi, gn_gain)


def _retention_constants():
    heads = jnp.arange(RET_HEADS, dtype=F32)
    log_gamma = jnp.log1p(-jnp.exp2(-5.0 - heads))
    idx = jnp.arange(RET_CHUNK, dtype=F32)
    rel = idx[:, None] - idx[None, :]
    inner = jnp.where(rel[None] >= 0,
                      jnp.exp(jnp.maximum(rel, 0.0)[None] * log_gamma[:, None, None]), 0.0)
    xi = jnp.exp((idx + 1.0)[None, :] * log_gamma[:, None])
    zeta = jnp.exp((RET_CHUNK - 1.0 - idx)[None, :] * log_gamma[:, None])
    return inner, xi, zeta


def _chunk_decay():
    return tuple(float((1.0 - 2.0 ** (-5 - hd)) ** RET_CHUNK) for hd in range(RET_HEADS))


def _sb_proj_kernel(h_ref, g_ref, w_ref, o_ref):
    width = 2 * MXU_WIDTH
    for part in _row_parts(h_ref.shape[0]):
        xn = _rms(h_ref[part, :], _gain(g_ref, 2)).astype(BF16)
        for c in range(SB_IN // width):
            cols = slice(c * width, (c + 1) * width)
            o_ref[part, cols] = _dot(xn, w_ref[:, cols]).astype(BF16)


def _sb_proj(h, gains, layer, w_in):
    n = h.shape[0]
    return pl.pallas_call(
        _sb_proj_kernel,
        grid=(n // TOK_TM,),
        in_specs=[
            pl.BlockSpec((TOK_TM, D_MODEL), lambda i: (i, 0)),
            pl.BlockSpec((None, N_NORMS, D_MODEL), lambda i: (layer, 0, 0)),
            _whole(w_in),
        ],
        out_specs=pl.BlockSpec((TOK_TM, SB_IN), lambda i: (i, 0)),
        out_shape=jax.ShapeDtypeStruct((n, SB_IN), BF16),
        compiler_params=_params(("parallel",)),
        name="sb_proj",
    )(h, gains, w_in)


def _sb_core_kernel(q_ref, k_ref, v_ref, o_ref, acc_ref, rest_ref):
    blk = SB_BLOCK
    seq = q_ref.shape[0]
    scale = SB_DH ** -0.5
    row = lax.broadcasted_iota(jnp.int32, (blk, blk), 0)
    col = lax.broadcasted_iota(jnp.int32, (blk, blk), 1)
    causal = col < row
    r2 = lax.broadcasted_iota(jnp.int32, (2 * blk, 2 * blk), 0) & (blk - 1)
    c2 = lax.broadcasted_iota(jnp.int32, (2 * blk, 2 * blk), 1)
    neg_suffix_ones = jnp.where((r2 >= c2) | (c2 >= blk), -1.0, 0.0).astype(BF16)
    contract_last = (((1,), (1,)), ((), ()))
    heads = range(SB_GROUP)
    cols = [slice(g * SB_DH, (g + 1) * SB_DH) for g in heads]

    def tiles(qblocks, nblk, diagonal):
        width = nblk * blk
        sub = [slice(c * blk, (c + 1) * blk) for c in range(nblk)]
        chains = [(slot, g, q0, k0) for slot, q0, k0 in qblocks for g in heads]
        count = len(chains)
        raws, zs, sums, worst = [None] * count, [None] * count, [None] * count, {}

        def scores(i):
            _, g, q0, k0 = chains[i]
            raws[i] = lax.dot_general(q_ref[pl.ds(q0, blk), cols[g]],
                                      k_ref[pl.ds(k0, width), cols[g]],
                                      contract_last, preferred_element_type=F32)

        def suffix_sums(i):
            zs[i] = raws[i] * scale
            decayed = jnp.exp2(jnp.abs(raws[i]) * (-scale * LOG2_E))
            softplus = jnp.maximum(zs[i], 0.0) + jnp.log(1.0 + decayed)
            pieces = [softplus[:, s] for s in sub]
            if diagonal:
                pieces[-1] = jnp.where(causal, pieces[-1], 0.0)
            split = []
            for piece in pieces:
                hi = piece.astype(BF16)
                split.append(jnp.concatenate(
                    [hi, (piece - hi.astype(F32)).astype(BF16)], axis=1))
            both = _dot(jnp.concatenate(split, axis=0), neg_suffix_ones)
            sums[i] = [both[s] for s in sub]

        def weigh(i):
            slot, g, _, k0 = chains[i]
            rest = None if diagonal else rest_ref[slot, g]
            weights = [None] * nblk
            for c in reversed(range(nblk)):
                logw = zs[i][:, sub[c]] + sums[i][c][:, :blk]
                if rest is not None:
                    logw = logw + rest
                a = jnp.exp(logw)
                if diagonal and c == nblk - 1:
                    a = jnp.where(causal, a, 0.0)
                weights[c] = a.astype(BF16)
                row_sum = sums[i][c][:, blk:]
                rest = row_sum if rest is None else rest + row_sum
            part = _dot(jnp.concatenate(weights, axis=1), v_ref[pl.ds(k0, width), cols[g]])
            if diagonal:
                acc_ref[slot, g] = part
            else:
                acc_ref[slot, g] += part
            rest_ref[slot, g] = rest
            worst[slot] = rest if slot not in worst else jnp.maximum(worst[slot], rest)

        for stage in (scores, suffix_sums, weigh):
            for i in range(count):
                stage(i)
        return [jnp.max(worst[slot]) for slot, _, _ in qblocks]

    def finish(slot, q0, j, worst):
        def more(c):
            j, worst = c
            return jnp.logical_and(j >= 0, worst > SB_SKIP_LOG)

        def step(c):
            j, _ = c
            return j - 1, tiles([(slot, q0, pl.multiple_of(j * blk, blk))], 1, False)[0]

        lax.while_loop(more, step, (j, worst))
        for g in heads:
            o_ref[pl.ds(q0, blk), cols[g]] = acc_ref[slot, g].astype(BF16)

    prefix = SB_FIRST_PASS - 1
    for qi in range(prefix):
        tiles([(0, qi * blk, 0)], qi + 1, True)
        for g in heads:
            o_ref[qi * blk:(qi + 1) * blk, cols[g]] = acc_ref[0, g].astype(BF16)

    def q_blocks(step, carry):
        first = prefix + step * SB_QBLOCKS
        qblocks = [(slot, pl.multiple_of((first + slot) * blk, blk),
                    pl.multiple_of((first + slot - prefix) * blk, blk))
                   for slot in range(SB_QBLOCKS)]
        worst = tiles(qblocks, SB_FIRST_PASS, True)
        for slot, q0, _ in qblocks:
            finish(slot, q0, first + slot - SB_FIRST_PASS, worst[slot])
        return carry

    lax.fori_loop(0, (seq // blk - prefix) // SB_QBLOCKS, q_blocks, 0)


def _sb_core(qkv, batch, seq):
    qkv = qkv.reshape(batch, seq, SB_IN)
    groups = SB_HEADS // SB_GROUP
    width = SB_GROUP * SB_DH
    heads = lambda part: pl.BlockSpec((None, seq, width),
                                      lambda b, g: (b, 0, part * groups + g))
    out = pl.pallas_call(
        _sb_core_kernel,
        grid=(batch, groups),
        in_specs=[heads(0), heads(1), heads(2)],
        out_specs=heads(0),
        out_shape=jax.ShapeDtypeStruct((batch, seq, SB_WIDTH), BF16),
        scratch_shapes=[pltpu.VMEM((SB_QBLOCKS, SB_GROUP, SB_BLOCK, SB_DH), F32),
                        pltpu.VMEM((SB_QBLOCKS, SB_GROUP, SB_BLOCK, SB_BLOCK), F32)],
        compiler_params=_params(("parallel", "parallel")),
        name="sb_core",
    )(qkv, qkv, qkv)
    return out.reshape(batch * seq, SB_WIDTH)


def kernel(x, p, positions, norm_gains, ffn_w_gate, ffn_w_up, ffn_w_down, ret_w_in,
           ret_gn_gain, ret_w_out, sb_w_in, sb_w_out, ple_w_gate, ple_w_proj):
    batch, seq, _ = x.shape
    n = batch * seq
    h = x.reshape(n, D_MODEL)
    p = p.reshape(DEPTH, n, PLE_DIM)
    gains = norm_gains
    ffn_stacks = (ffn_w_gate, ffn_w_up, ffn_w_down)
    ffn_weights = [w[0, 0].astype(BF16) for w in ffn_stacks]

    cos, sin = _rope_tables(positions)
    inner, xi, zeta = _retention_constants()
    xi_tab = jnp.broadcast_to(xi[:, :, None], (RET_HEADS, RET_CHUNK, RET_DV))
    zeta_tab = jnp.tile(jnp.repeat(zeta.T, RET_DK, axis=1), (RET_STEP // RET_CHUNK, 1))
    chunk_decay = _chunk_decay()

    for layer in range(DEPTH):
        j = layer // N_MIXERS
        retention = layer % N_MIXERS == 0
        mixer_stacks = (ret_w_in, ret_w_out) if retention else (sb_w_in, sb_w_out)
        h, cast = _ffn(h, gains, layer, ffn_weights,
                       [(w, (j,)) for w in mixer_stacks]
                       + [(w, (layer, 1)) for w in ffn_stacks]
                       + [(ple_w_gate, (layer,)), (ple_w_proj, (layer,))])
        w_in, w_out, ffn_weights, ple_weights = cast[0], cast[1], cast[2:5], cast[5:7]
        if retention:
            mixed = _retention(h, gains, layer, w_in, cos, sin, zeta_tab, inner, xi_tab,
                               ret_gn_gain[j].reshape(1, RET_V), chunk_decay, batch, seq)
        else:
            mixed = _sb_core(_sb_proj(h, gains, layer, w_in), batch, seq)
        next_ffn = [(w, (layer + 1, 0)) for w in ffn_stacks] if layer + 1 < DEPTH else []
        h, ffn_weights = _tail(h, mixed, p, gains, layer, w_out, ffn_weights, ple_weights,
                               next_ffn)
    return h.reshape(batch, seq, D_MODEL)
```

```python
import functools

import jax
import jax.numpy as jnp
from jax import lax
from jax.experimental import pallas as pl
from jax.experimental.pallas import tpu as pltpu

F32 = jnp.float32
BF16 = jnp.bfloat16

D_MODEL = 1024
DEPTH = 4
N_MIXERS = 2
PLE_DIM = 256
D_FF = 2816
FFN_RES_WEIGHT = 0.5
RET_HEADS = 4
RET_DK = D_MODEL // RET_HEADS
RET_QK = RET_HEADS * RET_DK
RET_DV = 2 * RET_DK
RET_V = RET_HEADS * RET_DV
RET_IN = 2 * RET_QK + 2 * RET_V
RET_CHUNK = 256
ROPE_BASE = 10000.0
ROPE_HALF = RET_DK // 2
GN_EPS = 1e-5
SB_HEADS = 8
SB_DH = D_MODEL // SB_HEADS
SB_WIDTH = SB_HEADS * SB_DH
SB_IN = 3 * SB_WIDTH
SB_BLOCK = 128
N_NORMS = 8
RMS_EPS = 1e-6
LOG2_E = 1.4426950408889634

VMEM_LIMIT_BYTES = 56 * 1024 * 1024

MXU_WIDTH = 256
FFN_TM = 512
FFN_TF = MXU_WIDTH
FFN_PARTS = 2
SB_GROUP = 4
SB_FIRST_PASS = 3
SB_QBLOCKS = 2
TOK_TM = 512
RET_STEP = 512

SB_SKIP_LOG = -105.0


def _params(sem):
    return pltpu.CompilerParams(dimension_semantics=sem,
                                vmem_limit_bytes=VMEM_LIMIT_BYTES)


def _resident(shape, index_map):
    return pl.BlockSpec(shape, index_map, pipeline_mode=pl.Buffered(1))


def _rms(x, g):
    ms = jnp.mean(x * x, axis=-1, keepdims=True)
    return x * lax.rsqrt(ms + RMS_EPS) * g


def _dot(a, b):
    return jnp.dot(a, b, preferred_element_type=F32)


def _silu(x):
    return x * jax.nn.sigmoid(x)


def _gain(gains_ref, k):
    return gains_ref[k:k + 1, :]


def _row_parts(rows):
    size = rows // FFN_PARTS
    return [slice(i * size, (i + 1) * size) for i in range(FFN_PARTS)]


def _half_ffn(parts, read_h, add_step, g_pre, g_post, wg_ref, wu_ref, wd_ref,
              xn_ref, hid_ref, side_work):
    def hidden(part, c):
        cols = slice(c * FFN_TF, (c + 1) * FFN_TF)
        xn = xn_ref[part, :]
        hid_ref[part, cols] = (_silu(_dot(xn, wg_ref[:, cols]))
                               * _dot(xn, wu_ref[:, cols])).astype(BF16)

    for i, part in enumerate(parts):
        xn_ref[part, :] = _rms(read_h(part), g_pre).astype(BF16)
        for c in range(D_FF // FFN_TF):
            hidden(part, c)
        if i == 0:
            side_work()
    g_step = g_post * FFN_RES_WEIGHT
    fs = [_dot(hid_ref[part, :], wd_ref[...]) for part in parts]
    for part, f in zip(parts, fs):
        add_step(part, _rms(f, g_step))


class _CastJobs:
    def __init__(self, jobs, steps):
        self.operands, self.in_specs, self.out_specs, self.out_shapes, self.shapes = [], [], [], [], []
        for w, lead in jobs:
            rows, cols = w.shape[-2:]
            slab = rows // steps
            assert slab * steps == rows and len(lead) == w.ndim - 2
            self.operands.append(w.reshape(w.shape[:-2] + (steps, slab, cols)))
            self.in_specs.append(pl.BlockSpec(
                (None,) * (len(lead) + 1) + (slab, cols),
                functools.partial(lambda lead, i: lead + (i, 0, 0), tuple(lead))))
            self.out_specs.append(pl.BlockSpec((None, slab, cols), lambda i: (i, 0, 0)))
            self.out_shapes.append(jax.ShapeDtypeStruct((steps, slab, cols), BF16))
            self.shapes.append((rows, cols))

    def __len__(self):
        return len(self.operands)

    def matrices(self, outputs):
        return [o.reshape(shape) for o, shape in zip(outputs, self.shapes)]


def _cast_slabs(src_refs, dst_refs):
    for src, dst in zip(src_refs, dst_refs):
        dst[...] = src[...].astype(BF16)


def _whole(w):
    return _resident(w.shape, lambda i: (0, 0))


def _ffn_kernel(n_cast, h_ref, gains_ref, wg_ref, wu_ref, wd_ref, *refs):
    cast_src, (o_ref, normed_ref, *cast_dst), (xn_ref, hid_ref) = (
        refs[:n_cast], refs[n_cast:2 * n_cast + 2], refs[2 * n_cast + 2:])

    def add_step(part, step):
        h = h_ref[part, :] + step
        o_ref[part, :] = h
        normed_ref[part, :] = _rms(h, _gain(gains_ref, 2)).astype(BF16)

    _half_ffn(_row_parts(h_ref.shape[0]), lambda part: h_ref[part, :], add_step,
              _gain(gains_ref, 0), _gain(gains_ref, 1), wg_ref, wu_ref, wd_ref,
              xn_ref, hid_ref, functools.partial(_cast_slabs, cast_src, cast_dst))


def _ffn_scratch():
    return [pltpu.VMEM((FFN_TM, D_MODEL), BF16), pltpu.VMEM((FFN_TM, D_FF), BF16)]


def _ffn(h, gains, layer, ffn_weights, cast_jobs):
    n = h.shape[0]
    steps = n // FFN_TM
    casts = _CastJobs(cast_jobs, steps)
    tile = pl.BlockSpec((FFN_TM, D_MODEL), lambda i: (i, 0))
    outs = pl.pallas_call(
        functools.partial(_ffn_kernel, len(casts)),
        grid=(steps,),
        in_specs=[
            tile,
            pl.BlockSpec((None, N_NORMS, D_MODEL), lambda i: (layer, 0, 0)),
        ] + [_whole(w) for w in ffn_weights] + casts.in_specs,
        out_specs=[tile, tile] + casts.out_specs,
        out_shape=[jax.ShapeDtypeStruct(h.shape, F32),
                   jax.ShapeDtypeStruct(h.shape, BF16)] + casts.out_shapes,
        scratch_shapes=_ffn_scratch(),
        compiler_params=_params(("parallel",)),
        name="ffn",
    )(h, gains, *ffn_weights, *casts.operands)
    return outs[0], outs[1], casts.matrices(outs[2:])


def _tail_kernel(n_cast, h_ref, mixed_ref, p_ref, gains_ref, wout_ref, wg_ref, wu_ref,
                 wd_ref, wgate_ref, wproj_ref, *refs):
    cast_src, (o_ref, *cast_dst), (xn_ref, hid_ref, h_scratch) = (
        refs[:n_cast], refs[n_cast:2 * n_cast + 1], refs[2 * n_cast + 1:])
    gain = functools.partial(_gain, gains_ref)
    parts = _row_parts(h_ref.shape[0])
    mixer_out = {part.start: _dot(mixed_ref[part, :], wout_ref[...]) for part in parts}

    def after_mixer(part):
        h = h_ref[part, :] + _rms(mixer_out[part.start], gain(3))
        h_scratch[part, :] = h
        return h

    def add_step_and_embedding(part, step):
        h = h_scratch[part, :] + step
        gate = jax.nn.sigmoid(_dot(_rms(h, gain(6)).astype(BF16), wgate_ref[...]))
        e = _dot(p_ref[part, :].astype(BF16), wproj_ref[...])
        o_ref[part, :] = h + _rms(gate * e, gain(7))

    _half_ffn(parts, after_mixer, add_step_and_embedding, gain(4), gain(5),
              wg_ref, wu_ref, wd_ref, xn_ref, hid_ref,
              functools.partial(_cast_slabs, cast_src, cast_dst))


def _tail(h, mixed, p, gains, layer, w_out, ffn_weights, ple_weights, cast_jobs):
    n = h.shape[0]
    steps = n // FFN_TM
    casts = _CastJobs(cast_jobs, steps)
    weights = [w_out, *ffn_weights, *ple_weights]
    outs = pl.pallas_call(
        functools.partial(_tail_kernel, len(casts)),
        grid=(steps,),
        in_specs=[
            pl.BlockSpec((FFN_TM, D_MODEL), lambda i: (i, 0)),
            pl.BlockSpec((FFN_TM, mixed.shape[1]), lambda i: (i, 0)),
            pl.BlockSpec((None, FFN_TM, PLE_DIM), lambda i: (layer, i, 0)),
            pl.BlockSpec((None, N_NORMS, D_MODEL), lambda i: (layer, 0, 0)),
        ] + [_whole(w) for w in weights] + casts.in_specs,
        out_specs=[pl.BlockSpec((FFN_TM, D_MODEL), lambda i: (i, 0))] + casts.out_specs,
        out_shape=[jax.ShapeDtypeStruct(h.shape, F32)] + casts.out_shapes,
        scratch_shapes=_ffn_scratch() + [pltpu.VMEM((FFN_TM, D_MODEL), F32)],
        compiler_params=_params(("parallel",)),
        name="tail",
    )(h, mixed, p, gains, *weights, *casts.operands)
    return outs[0], casts.matrices(outs[1:])


def _rope_table_kernel(pos_ref, inv_ref, cos_ref, sin_ref):
    ang = pos_ref[...].astype(F32) * inv_ref[...]
    cos_ref[...] = jnp.cos(ang)
    sin_ref[...] = jnp.sin(ang)


def _rope_tables(positions):
    n = positions.size
    tm = 1024
    inv = ROPE_BASE ** (-jnp.arange(ROPE_HALF, dtype=F32) / ROPE_HALF)
    out = jax.ShapeDtypeStruct((n, ROPE_HALF), F32)
    return pl.pallas_call(
        _rope_table_kernel,
        grid=(n // tm,),
        in_specs=[pl.BlockSpec((tm, 1), lambda i: (i, 0)),
                  pl.BlockSpec((1, ROPE_HALF), lambda i: (0, 0))],
        out_specs=[pl.BlockSpec((tm, ROPE_HALF), lambda i: (i, 0))] * 2,
        out_shape=[out, out],
        compiler_params=_params(("parallel",)),
        name="rope_tables",
    )(positions.reshape(n, 1), inv.reshape(1, ROPE_HALF))


def _retention_kernel(chunk_decay, xn_ref, w_ref, cos_ref, sin_ref, zeta_ref,
                      decay_ref, xi_ref, gn_gain_ref, o_ref,
                      state_ref, q_ref, k_ref, kz_ref, v_ref, gate_ref):
    @pl.when(pl.program_id(1) == 0)
    def _():
        state_ref[...] = jnp.zeros_like(state_ref)

    k_scale = RET_DK ** -0.5
    contract_last = (((1,), (1,)), ((), ()))
    contract_first = (((0,), (0,)), ((), ()))

    def projection_steps(hd):
        lo = hd * RET_DK
        mid = lo + ROPE_HALF
        hi = lo + RET_DK

        def rotated(col):
            t = _dot(xn_ref[...], w_ref[:, col:col + RET_DK])
            t1, t2 = t[:, :ROPE_HALF], t[:, ROPE_HALF:]
            cos = cos_ref[...]
            sin = sin_ref[...]
            return t1 * cos - t2 * sin, t1 * sin + t2 * cos

        def queries():
            q1, q2 = rotated(lo)
            q_ref[:, lo:mid] = q1.astype(BF16)
            q_ref[:, mid:hi] = q2.astype(BF16)

        def keys():
            k1, k2 = rotated(RET_QK + lo)
            k1 = k1 * k_scale
            k2 = k2 * k_scale
            k_ref[:, lo:mid] = k1.astype(BF16)
            k_ref[:, mid:hi] = k2.astype(BF16)
            kz_ref[:, lo:mid] = (k1 * zeta_ref[:, lo:mid]).astype(BF16)
            kz_ref[:, mid:hi] = (k2 * zeta_ref[:, mid:hi]).astype(BF16)

        def values(c):
            cols = slice(hd * RET_DV + c * MXU_WIDTH, hd * RET_DV + (c + 1) * MXU_WIDTH)
            v0 = 2 * RET_QK + cols.start
            v_ref[:, cols] = _dot(xn_ref[...], w_ref[:, v0:v0 + MXU_WIDTH]).astype(BF16)

        def gates(c):
            cols = slice(hd * RET_DV + c * MXU_WIDTH, hd * RET_DV + (c + 1) * MXU_WIDTH)
            g0 = 2 * RET_QK + RET_V + cols.start
            gate_ref[:, cols] = _dot(xn_ref[...], w_ref[:, g0:g0 + MXU_WIDTH])

        halves = range(RET_DV // MXU_WIDTH)
        return ([queries, keys] + [functools.partial(values, c) for c in halves]
                + [functools.partial(gates, c) for c in halves])

    def recurrence_steps(hd):
        return [functools.partial(recur, hd, c) for c in range(RET_STEP // RET_CHUNK)]

    def recur(hd, c):
        qk_cols = slice(hd * RET_DK, (hd + 1) * RET_DK)
        v_cols = slice(hd * RET_DV, (hd + 1) * RET_DV)
        rows = slice(c * RET_CHUNK, (c + 1) * RET_CHUNK)
        q = q_ref[rows, qk_cols]
        v = v_ref[rows, v_cols]
        scores = lax.dot_general(q, k_ref[rows, qk_cols], contract_last,
                                 preferred_element_type=F32) * decay_ref[hd]
        state = state_ref[hd]
        o = _dot(scores.astype(BF16), v) + _dot(q, state.astype(BF16)) * xi_ref[hd]
        state_ref[hd] = state * chunk_decay[hd] + lax.dot_general(
            kz_ref[rows, qk_cols], v, contract_first, preferred_element_type=F32)
        mu = jnp.mean(o, axis=-1, keepdims=True)
        d = o - mu
        var = jnp.mean(d * d, axis=-1, keepdims=True)
        normed = d * lax.rsqrt(var + GN_EPS) * gn_gain_ref[:, v_cols]
        o_ref[rows, v_cols] = (_silu(gate_ref[rows, v_cols]) * normed).astype(BF16)

    for step in projection_steps(0):
        step()
    for hd in range(RET_HEADS):
        ahead = projection_steps(hd + 1) if hd + 1 < RET_HEADS else []
        chunks = recurrence_steps(hd)
        for i in range(max(len(ahead), len(chunks))):
            for steps in (ahead, chunks):
                if i < len(steps):
                    steps[i]()


def _retention(xn, w_in, cos, sin, zeta_tab, decay, xi, gn_gain, chunk_decay, batch, seq):
    n = xn.shape[0]
    steps = seq // RET_STEP
    tok = lambda width: pl.BlockSpec((RET_STEP, width), lambda b, t: (b * steps + t, 0))
    staged = lambda width, dtype: pltpu.VMEM((RET_STEP, width), dtype)
    return pl.pallas_call(
        functools.partial(_retention_kernel, chunk_decay),
        grid=(batch, steps),
        in_specs=[
            tok(D_MODEL),
            _resident(w_in.shape, lambda b, t: (0, 0)),
            tok(ROPE_HALF), tok(ROPE_HALF),
            _resident((RET_STEP, RET_QK), lambda b, t: (0, 0)),
            _resident((RET_HEADS, RET_CHUNK, RET_CHUNK), lambda b, t: (0, 0, 0)),
            _resident((RET_HEADS, RET_CHUNK, RET_DV), lambda b, t: (0, 0, 0)),
            pl.BlockSpec((1, RET_V), lambda b, t: (0, 0)),
        ],
        out_specs=tok(RET_V),
        out_shape=jax.ShapeDtypeStruct((n, RET_V), BF16),
        scratch_shapes=[pltpu.VMEM((RET_HEADS, RET_DK, RET_DV), F32),
                        staged(RET_QK, BF16), staged(RET_QK, BF16),
                        staged(RET_QK, BF16), staged(RET_V, BF16), staged(RET_V, F32)],
        compiler_params=_params(("parallel", "arbitrary")),
        name="retention",
    )(xn, w_in, cos, sin, zeta_tab, decay, xi, gn_gain)


def _retention_constants():
    heads = jnp.arange(RET_HEADS, dtype=F32)
    log_gamma = jnp.log1p(-jnp.exp2(-5.0 - heads))
    idx = jnp.arange(RET_CHUNK, dtype=F32)
    rel = idx[:, None] - idx[None, :]
    inner = jnp.where(rel[None] >= 0,
                      jnp.exp(jnp.maximum(rel, 0.0)[None] * log_gamma[:, None, None]), 0.0)
    xi = jnp.exp((idx + 1.0)[None, :] * log_gamma[:, None])
    zeta = jnp.exp((RET_CHUNK - 1.0 - idx)[None, :] * log_gamma[:, None])
    return inner, xi, zeta


def _chunk_decay():
    return tuple(float((1.0 - 2.0 ** (-5 - hd)) ** RET_CHUNK) for hd in range(RET_HEADS))


def _stick_breaking_kernel(xn_ref, wq_ref, wk_ref, wv_ref, o_ref,
                           q_ref, k_ref, v_ref, qnew_ref, knew_ref, vnew_ref,
                           acc_ref, rest_ref):
    blk = SB_BLOCK
    seq = q_ref.shape[0]

    projections = ((wq_ref, q_ref, qnew_ref), (wk_ref, k_ref, knew_ref),
                   (wv_ref, v_ref, vnew_ref))

    ahead_rows = qnew_ref.shape[0]

    def project(r0):
        x = xn_ref[pl.ds(r0, ahead_rows), :]
        for w_ref, _, new_ref in projections:
            new_ref[...] = _dot(x, w_ref[...]).astype(BF16)

    def commit(r0):
        for _, dst, new_ref in projections:
            dst[pl.ds(r0, ahead_rows), :] = new_ref[...]

    scale = SB_DH ** -0.5
    row = lax.broadcasted_iota(jnp.int32, (blk, blk), 0)
    col = lax.broadcasted_iota(jnp.int32, (blk, blk), 1)
    causal = col < row
    r2 = lax.broadcasted_iota(jnp.int32, (2 * blk, 2 * blk), 0) & (blk - 1)
    c2 = lax.broadcasted_iota(jnp.int32, (2 * blk, 2 * blk), 1)
    neg_suffix_ones = jnp.where((r2 >= c2) | (c2 >= blk), -1.0, 0.0).astype(BF16)
    contract_last = (((1,), (1,)), ((), ()))
    heads = range(SB_GROUP)
    cols = [slice(g * SB_DH, (g + 1) * SB_DH) for g in heads]

    def tiles(qblocks, nblk, diagonal, side_work=None):
        width = nblk * blk
        sub = [slice(c * blk, (c + 1) * blk) for c in range(nblk)]
        chains = [(slot, g, q0, k0) for slot, q0, k0 in qblocks for g in heads]
        count = len(chains)
        raws, zs, sums, worst = [None] * count, [None] * count, [None] * count, {}

        def scores(i):
            _, g, q0, k0 = chains[i]
            raws[i] = lax.dot_general(q_ref[pl.ds(q0, blk), cols[g]],
                                      k_ref[pl.ds(k0, width), cols[g]],
                                      contract_last, preferred_element_type=F32)

        def suffix_sums(i):
            zs[i] = raws[i] * scale
            decayed = jnp.exp2(jnp.abs(raws[i]) * (-scale * LOG2_E))
            softplus = jnp.maximum(zs[i], 0.0) + jnp.log(1.0 + decayed)
            pieces = [softplus[:, s] for s in sub]
            if diagonal:
                pieces[-1] = jnp.where(causal, pieces[-1], 0.0)
            split = []
            for piece in pieces:
                hi = piece.astype(BF16)
                split.append(jnp.concatenate(
                    [hi, (piece - hi.astype(F32)).astype(BF16)], axis=1))
            both = _dot(jnp.concatenate(split, axis=0), neg_suffix_ones)
            sums[i] = [both[s] for s in sub]

        def weigh(i):
            slot, g, _, k0 = chains[i]
            rest = None if diagonal else rest_ref[slot, g]
            weights = [None] * nblk
            for c in reversed(range(nblk)):
                logw = zs[i][:, sub[c]] + sums[i][c][:, :blk]
                if rest is not None:
                    logw = logw + rest
                a = jnp.exp(logw)
                if diagonal and c == nblk - 1:
                    a = jnp.where(causal, a, 0.0)
                weights[c] = a.astype(BF16)
                row_sum = sums[i][c][:, blk:]
                rest = row_sum if rest is None else rest + row_sum
            part = _dot(jnp.concatenate(weights, axis=1), v_ref[pl.ds(k0, width), cols[g]])
            if diagonal:
                acc_ref[slot, g] = part
            else:
                acc_ref[slot, g] += part
            rest_ref[slot, g] = rest
            worst[slot] = rest if slot not in worst else jnp.maximum(worst[slot], rest)

        for i in range(count):
            scores(i)
        if side_work is not None:
            side_work()
        for stage in (suffix_sums, weigh):
            for i in range(count):
                stage(i)
        return [jnp.max(worst[slot]) for slot, _, _ in qblocks]

    def finish(slot, q0, j, worst):
        def more(c):
            j, worst = c
            return jnp.logical_and(j >= 0, worst > SB_SKIP_LOG)

        def step(c):
            j, _ = c
            return j - 1, tiles([(slot, q0, pl.multiple_of(j * blk, blk))], 1, False)[0]

        lax.while_loop(more, step, (j, worst))
        for g in heads:
            o_ref[pl.ds(q0, blk), cols[g]] = acc_ref[slot, g].astype(BF16)

    nq = seq // blk
    prefix = SB_FIRST_PASS - 1
    for r0 in range(0, (prefix + SB_QBLOCKS) * blk, ahead_rows):
        project(r0)
        commit(r0)
    for qi in range(prefix):
        tiles([(0, qi * blk, 0)], qi + 1, True)
        for g in heads:
            o_ref[qi * blk:(qi + 1) * blk, cols[g]] = acc_ref[0, g].astype(BF16)

    def q_blocks(step, carry):
        first = prefix + step * SB_QBLOCKS
        qblocks = [(slot, pl.multiple_of((first + slot) * blk, blk),
                    pl.multiple_of((first + slot - prefix) * blk, blk))
                   for slot in range(SB_QBLOCKS)]
        ahead = pl.multiple_of(
            jnp.minimum(first + SB_QBLOCKS, nq - SB_QBLOCKS) * blk, blk)
        worst = tiles(qblocks, SB_FIRST_PASS, True,
                      side_work=functools.partial(project, ahead))
        for slot, q0, _ in qblocks:
            finish(slot, q0, first + slot - SB_FIRST_PASS, worst[slot])
        commit(ahead)
        return carry

    lax.fori_loop(0, (nq - prefix) // SB_QBLOCKS, q_blocks, 0)


def _stick_breaking(xn, w_in, batch, seq):
    xn = xn.reshape(batch, seq, D_MODEL)
    groups = SB_HEADS // SB_GROUP
    width = SB_GROUP * SB_DH
    weight = lambda part: pl.BlockSpec((D_MODEL, width), lambda b, g: (0, part * groups + g))
    staged = pltpu.VMEM((seq, width), BF16)
    ahead = pltpu.VMEM((SB_QBLOCKS * SB_BLOCK, width), BF16)
    out = pl.pallas_call(
        _stick_breaking_kernel,
        grid=(batch, groups),
        in_specs=[pl.BlockSpec((None, seq, D_MODEL), lambda b, g: (b, 0, 0)),
                  weight(0), weight(1), weight(2)],
        out_specs=pl.BlockSpec((None, seq, width), lambda b, g: (b, 0, g)),
        out_shape=jax.ShapeDtypeStruct((batch, seq, SB_WIDTH), BF16),
        scratch_shapes=[staged, staged, staged, ahead, ahead, ahead,
                        pltpu.VMEM((SB_QBLOCKS, SB_GROUP, SB_BLOCK, SB_DH), F32),
                        pltpu.VMEM((SB_QBLOCKS, SB_GROUP, SB_BLOCK, SB_BLOCK), F32)],
        compiler_params=_params(("parallel", "arbitrary")),
        name="stick_breaking",
    )(xn, w_in, w_in, w_in)
    return out.reshape(batch * seq, SB_WIDTH)


def kernel(x, p, positions, norm_gains, ffn_w_gate, ffn_w_up, ffn_w_down, ret_w_in,
           ret_gn_gain, ret_w_out, sb_w_in, sb_w_out, ple_w_gate, ple_w_proj):
    batch, seq, _ = x.shape
    n = batch * seq
    h = x.reshape(n, D_MODEL)
    p = p.reshape(DEPTH, n, PLE_DIM)
    gains = norm_gains
    ffn_stacks = (ffn_w_gate, ffn_w_up, ffn_w_down)
    ffn_weights = [w[0, 0].astype(BF16) for w in ffn_stacks]

    cos, sin = _rope_tables(positions)
    inner, xi, zeta = _retention_constants()
    xi_tab = jnp.broadcast_to(xi[:, :, None], (RET_HEADS, RET_CHUNK, RET_DV))
    zeta_tab = jnp.tile(jnp.repeat(zeta.T, RET_DK, axis=1), (RET_STEP // RET_CHUNK, 1))
    chunk_decay = _chunk_decay()

    for layer in range(DEPTH):
        j = layer // N_MIXERS
        retention = layer % N_MIXERS == 0
        mixer_stacks = (ret_w_in, ret_w_out) if retention else (sb_w_in, sb_w_out)
        h, xn, cast = _ffn(h, gains, layer, ffn_weights,
                           [(w, (j,)) for w in mixer_stacks]
                           + [(w, (layer, 1)) for w in ffn_stacks]
                           + [(ple_w_gate, (layer,)), (ple_w_proj, (layer,))])
        w_in, w_out, ffn_weights, ple_weights = cast[0], cast[1], cast[2:5], cast[5:7]
        if retention:
            mixed = _retention(xn, w_in, cos, sin, zeta_tab, inner, xi_tab,
                               ret_gn_gain[j].reshape(1, RET_V), chunk_decay, batch, seq)
        else:
            mixed = _stick_breaking(xn, w_in, batch, seq)
        next_ffn = [(w, (layer + 1, 0)) for w in ffn_stacks] if layer + 1 < DEPTH else []
        h, ffn_weights = _tail(h, mixed, p, gains, layer, w_out, ffn_weights, ple_weights,
                               next_ffn)
    return h.reshape(batch, seq, D_MODEL)
```

```python
import functools

import jax
import jax.numpy as jnp
from jax import lax
from jax.experimental import pallas as pl
from jax.experimental.pallas import tpu as pltpu

F32 = jnp.float32
BF16 = jnp.bfloat16

D_MODEL = 1024
DEPTH = 4
N_MIXERS = 2
PLE_DIM = 256
D_FF = 2816
FFN_RES_WEIGHT = 0.5
RET_HEADS = 4
RET_DK = D_MODEL // RET_HEADS
RET_QK = RET_HEADS * RET_DK
RET_DV = 2 * RET_DK
RET_V = RET_HEADS * RET_DV
RET_IN = 2 * RET_QK + 2 * RET_V
RET_CHUNK = 256
ROPE_BASE = 10000.0
ROPE_HALF = RET_DK // 2
GN_EPS = 1e-5
SB_HEADS = 8
SB_DH = D_MODEL // SB_HEADS
SB_WIDTH = SB_HEADS * SB_DH
SB_IN = 3 * SB_WIDTH
SB_BLOCK = 128
N_NORMS = 8
RMS_EPS = 1e-6
LOG2_E = 1.4426950408889634

VMEM_LIMIT_BYTES = 56 * 1024 * 1024

MXU_WIDTH = 256
FFN_TM = 512
FFN_TF = MXU_WIDTH
FFN_PART_WEIGHTS = (1, 1)
SB_GROUP = 4
SB_FIRST_PASS = 3
SB_QBLOCKS = 2
TOK_TM = 512
RET_STEP = 512

SB_SKIP_LOG = -105.0


def _params(sem):
    return pltpu.CompilerParams(dimension_semantics=sem,
                                vmem_limit_bytes=VMEM_LIMIT_BYTES)


def _resident(shape, index_map):
    return pl.BlockSpec(shape, index_map, pipeline_mode=pl.Buffered(1))


def _rms(x, g):
    ms = jnp.mean(x * x, axis=-1, keepdims=True)
    return x * lax.rsqrt(ms + RMS_EPS) * g


def _dot(a, b):
    return jnp.dot(a, b, preferred_element_type=F32)


def _silu(x):
    return x * jax.nn.sigmoid(x)


def _gain(gains_ref, k):
    return gains_ref[k:k + 1, :]


def _row_parts(rows):
    bounds = [0] + [rows * sum(FFN_PART_WEIGHTS[:i + 1]) // sum(FFN_PART_WEIGHTS)
                    for i in range(len(FFN_PART_WEIGHTS))]
    return [slice(lo, hi) for lo, hi in zip(bounds[:-1], bounds[1:])]


def _half_ffn(parts, read_h, add_step, g_pre, g_post, wg_ref, wu_ref, wd_ref,
              xn_ref, hid_ref, side_work):
    def hidden(part, c):
        cols = slice(c * FFN_TF, (c + 1) * FFN_TF)
        xn = xn_ref[part, :]
        hid_ref[part, cols] = (_silu(_dot(xn, wg_ref[:, cols]))
                               * _dot(xn, wu_ref[:, cols])).astype(BF16)

    for i, part in enumerate(parts):
        xn_ref[part, :] = _rms(read_h(part), g_pre).astype(BF16)
        for c in range(D_FF // FFN_TF):
            hidden(part, c)
        if i == 0:
            side_work()
    g_step = g_post * FFN_RES_WEIGHT
    fs = [_dot(hid_ref[part, :], wd_ref[...]) for part in parts]
    for part, f in zip(parts, fs):
        add_step(part, _rms(f, g_step))


class _CastJobs:
    def __init__(self, jobs, steps):
        self.operands, self.in_specs, self.out_specs, self.out_shapes, self.shapes = [], [], [], [], []
        for w, lead in jobs:
            rows, cols = w.shape[-2:]
            slab = rows // steps
            assert slab * steps == rows and len(lead) == w.ndim - 2
            self.operands.append(w.reshape(w.shape[:-2] + (steps, slab, cols)))
            self.in_specs.append(pl.BlockSpec(
                (None,) * (len(lead) + 1) + (slab, cols),
                functools.partial(lambda lead, i: lead + (i, 0, 0), tuple(lead))))
            self.out_specs.append(pl.BlockSpec((None, slab, cols), lambda i: (i, 0, 0)))
            self.out_shapes.append(jax.ShapeDtypeStruct((steps, slab, cols), BF16))
            self.shapes.append((rows, cols))

    def __len__(self):
        return len(self.operands)

    def matrices(self, outputs):
        return [o.reshape(shape) for o, shape in zip(outputs, self.shapes)]


def _cast_slabs(src_refs, dst_refs):
    for src, dst in zip(src_refs, dst_refs):
        dst[...] = src[...].astype(BF16)


def _whole(w):
    return _resident(w.shape, lambda i: (0, 0))


def _ffn_kernel(n_cast, h_ref, gains_ref, wg_ref, wu_ref, wd_ref, *refs):
    cast_src, (o_ref, normed_ref, *cast_dst), (xn_ref, hid_ref) = (
        refs[:n_cast], refs[n_cast:2 * n_cast + 2], refs[2 * n_cast + 2:])

    def add_step(part, step):
        h = h_ref[part, :] + step
        o_ref[part, :] = h
        normed_ref[part, :] = _rms(h, _gain(gains_ref, 2)).astype(BF16)

    _half_ffn(_row_parts(h_ref.shape[0]), lambda part: h_ref[part, :], add_step,
              _gain(gains_ref, 0), _gain(gains_ref, 1), wg_ref, wu_ref, wd_ref,
              xn_ref, hid_ref, functools.partial(_cast_slabs, cast_src, cast_dst))


def _ffn_scratch():
    return [pltpu.VMEM((FFN_TM, D_MODEL), BF16), pltpu.VMEM((FFN_TM, D_FF), BF16)]


def _ffn(h, gains, layer, ffn_weights, cast_jobs):
    n = h.shape[0]
    steps = n // FFN_TM
    casts = _CastJobs(cast_jobs, steps)
    tile = pl.BlockSpec((FFN_TM, D_MODEL), lambda i: (i, 0))
    outs = pl.pallas_call(
        functools.partial(_ffn_kernel, len(casts)),
        grid=(steps,),
        in_specs=[
            tile,
            pl.BlockSpec((None, N_NORMS, D_MODEL), lambda i: (layer, 0, 0)),
        ] + [_whole(w) for w in ffn_weights] + casts.in_specs,
        out_specs=[tile, tile] + casts.out_specs,
        out_shape=[jax.ShapeDtypeStruct(h.shape, F32),
                   jax.ShapeDtypeStruct(h.shape, BF16)] + casts.out_shapes,
        scratch_shapes=_ffn_scratch(),
        compiler_params=_params(("parallel",)),
        name="ffn",
    )(h, gains, *ffn_weights, *casts.operands)
    return outs[0], outs[1], casts.matrices(outs[2:])


def _tail_kernel(n_cast, h_ref, mixed_ref, p_ref, gains_ref, wout_ref, wg_ref, wu_ref,
                 wd_ref, wgate_ref, wproj_ref, *refs):
    cast_src, (o_ref, *cast_dst), (xn_ref, hid_ref, h_scratch) = (
        refs[:n_cast], refs[n_cast:2 * n_cast + 1], refs[2 * n_cast + 1:])
    gain = functools.partial(_gain, gains_ref)
    parts = _row_parts(h_ref.shape[0])
    mixer_out = {part.start: _dot(mixed_ref[part, :], wout_ref[...]) for part in parts}

    def after_mixer(part):
        h = h_ref[part, :] + _rms(mixer_out[part.start], gain(3))
        h_scratch[part, :] = h
        return h

    def add_step_and_embedding(part, step):
        h = h_scratch[part, :] + step
        gate = jax.nn.sigmoid(_dot(_rms(h, gain(6)).astype(BF16), wgate_ref[...]))
        e = _dot(p_ref[part, :].astype(BF16), wproj_ref[...])
        o_ref[part, :] = h + _rms(gate * e, gain(7))

    _half_ffn(parts, after_mixer, add_step_and_embedding, gain(4), gain(5),
              wg_ref, wu_ref, wd_ref, xn_ref, hid_ref,
              functools.partial(_cast_slabs, cast_src, cast_dst))


def _tail(h, mixed, p, gains, layer, w_out, ffn_weights, ple_weights, cast_jobs):
    n = h.shape[0]
    steps = n // FFN_TM
    casts = _CastJobs(cast_jobs, steps)
    weights = [w_out, *ffn_weights, *ple_weights]
    outs = pl.pallas_call(
        functools.partial(_tail_kernel, len(casts)),
        grid=(steps,),
        in_specs=[
            pl.BlockSpec((FFN_TM, D_MODEL), lambda i: (i, 0)),
            pl.BlockSpec((FFN_TM, mixed.shape[1]), lambda i: (i, 0)),
            pl.BlockSpec((None, FFN_TM, PLE_DIM), lambda i: (layer, i, 0)),
            pl.BlockSpec((None, N_NORMS, D_MODEL), lambda i: (layer, 0, 0)),
        ] + [_whole(w) for w in weights] + casts.in_specs,
        out_specs=[pl.BlockSpec((FFN_TM, D_MODEL), lambda i: (i, 0))] + casts.out_specs,
        out_shape=[jax.ShapeDtypeStruct(h.shape, F32)] + casts.out_shapes,
        scratch_shapes=_ffn_scratch() + [pltpu.VMEM((FFN_TM, D_MODEL), F32)],
        compiler_params=_params(("parallel",)),
        name="tail",
    )(h, mixed, p, gains, *weights, *casts.operands)
    return outs[0], casts.matrices(outs[1:])


def _rope_table_kernel(n_cast, pos_ref, inv_ref, *refs):
    cast_src, (cos_ref, sin_ref, *cast_dst) = refs[:n_cast], refs[n_cast:]
    ang = pos_ref[...].astype(F32) * inv_ref[...]
    cos_ref[...] = jnp.cos(ang)
    sin_ref[...] = jnp.sin(ang)
    _cast_slabs(cast_src, cast_dst)


def _rope_tables(positions, cast_jobs):
    n = positions.size
    tm = 1024
    casts = _CastJobs(cast_jobs, n // tm)
    inv = ROPE_BASE ** (-jnp.arange(ROPE_HALF, dtype=F32) / ROPE_HALF)
    out = jax.ShapeDtypeStruct((n, ROPE_HALF), F32)
    outs = pl.pallas_call(
        functools.partial(_rope_table_kernel, len(casts)),
        grid=(n // tm,),
        in_specs=[pl.BlockSpec((tm, 1), lambda i: (i, 0)),
                  pl.BlockSpec((1, ROPE_HALF), lambda i: (0, 0))] + casts.in_specs,
        out_specs=[pl.BlockSpec((tm, ROPE_HALF), lambda i: (i, 0))] * 2 + casts.out_specs,
        out_shape=[out, out] + casts.out_shapes,
        compiler_params=_params(("parallel",)),
        name="rope_tables",
    )(positions.reshape(n, 1), inv.reshape(1, ROPE_HALF), *casts.operands)
    return outs[0], outs[1], casts.matrices(outs[2:])


def _retention_kernel(chunk_decay, xn_ref, w_ref, cos_ref, sin_ref, zeta_ref,
                      decay_ref, xi_ref, gn_gain_ref, o_ref,
                      state_ref, q_ref, k_ref, kz_ref, v_ref, gate_ref):
    @pl.when(pl.program_id(1) == 0)
    def _():
        state_ref[...] = jnp.zeros_like(state_ref)

    k_scale = RET_DK ** -0.5
    contract_last = (((1,), (1,)), ((), ()))
    contract_first = (((0,), (0,)), ((), ()))

    def projection_steps(hd):
        lo = hd * RET_DK
        mid = lo + ROPE_HALF
        hi = lo + RET_DK

        def rotated(col):
            t = _dot(xn_ref[...], w_ref[:, col:col + RET_DK])
            t1, t2 = t[:, :ROPE_HALF], t[:, ROPE_HALF:]
            cos = cos_ref[...]
            sin = sin_ref[...]
            return t1 * cos - t2 * sin, t1 * sin + t2 * cos

        def queries():
            q1, q2 = rotated(lo)
            q_ref[:, lo:mid] = q1.astype(BF16)
            q_ref[:, mid:hi] = q2.astype(BF16)

        def keys():
            k1, k2 = rotated(RET_QK + lo)
            k1 = k1 * k_scale
            k2 = k2 * k_scale
            k_ref[:, lo:mid] = k1.astype(BF16)
            k_ref[:, mid:hi] = k2.astype(BF16)
            kz_ref[:, lo:mid] = (k1 * zeta_ref[:, lo:mid]).astype(BF16)
            kz_ref[:, mid:hi] = (k2 * zeta_ref[:, mid:hi]).astype(BF16)

        def values(c):
            cols = slice(hd * RET_DV + c * MXU_WIDTH, hd * RET_DV + (c + 1) * MXU_WIDTH)
            v0 = 2 * RET_QK + cols.start
            v_ref[:, cols] = _dot(xn_ref[...], w_ref[:, v0:v0 + MXU_WIDTH]).astype(BF16)

        def gates(c):
            cols = slice(hd * RET_DV + c * MXU_WIDTH, hd * RET_DV + (c + 1) * MXU_WIDTH)
            g0 = 2 * RET_QK + RET_V + cols.start
            gate_ref[:, cols] = _dot(xn_ref[...], w_ref[:, g0:g0 + MXU_WIDTH])

        halves = range(RET_DV // MXU_WIDTH)
        return ([queries, keys] + [functools.partial(values, c) for c in halves]
                + [functools.partial(gates, c) for c in halves])

    def recurrence_steps(hd):
        return [functools.partial(recur, hd, c) for c in range(RET_STEP // RET_CHUNK)]

    def recur(hd, c):
        qk_cols = slice(hd * RET_DK, (hd + 1) * RET_DK)
        v_cols = slice(hd * RET_DV, (hd + 1) * RET_DV)
        rows = slice(c * RET_CHUNK, (c + 1) * RET_CHUNK)
        q = q_ref[rows, qk_cols]
        v = v_ref[rows, v_cols]
        scores = lax.dot_general(q, k_ref[rows, qk_cols], contract_last,
                                 preferred_element_type=F32) * decay_ref[hd]
        state = state_ref[hd]
        o = _dot(scores.astype(BF16), v) + _dot(q, state.astype(BF16)) * xi_ref[hd]
        state_ref[hd] = state * chunk_decay[hd] + lax.dot_general(
            kz_ref[rows, qk_cols], v, contract_first, preferred_element_type=F32)
        mu = jnp.mean(o, axis=-1, keepdims=True)
        d = o - mu
        var = jnp.mean(d * d, axis=-1, keepdims=True)
        normed = d * lax.rsqrt(var + GN_EPS) * gn_gain_ref[:, v_cols]
        o_ref[rows, v_cols] = (_silu(gate_ref[rows, v_cols]) * normed).astype(BF16)

    for step in projection_steps(0):
        step()
    for hd in range(RET_HEADS):
        ahead = projection_steps(hd + 1) if hd + 1 < RET_HEADS else []
        chunks = recurrence_steps(hd)
        for i in range(max(len(ahead), len(chunks))):
            for steps in (ahead, chunks):
                if i < len(steps):
                    steps[i]()


def _retention(xn, w_in, cos, sin, zeta_tab, decay, xi, gn_gain, chunk_decay, batch, seq):
    n = xn.shape[0]
    steps = seq // RET_STEP
    tok = lambda width: pl.BlockSpec((RET_STEP, width), lambda b, t: (b * steps + t, 0))
    staged = lambda width, dtype: pltpu.VMEM((RET_STEP, width), dtype)
    return pl.pallas_call(
        functools.partial(_retention_kernel, chunk_decay),
        grid=(batch, steps),
        in_specs=[
            tok(D_MODEL),
            _resident(w_in.shape, lambda b, t: (0, 0)),
            tok(ROPE_HALF), tok(ROPE_HALF),
            _resident((RET_STEP, RET_QK), lambda b, t: (0, 0)),
            _resident((RET_HEADS, RET_CHUNK, RET_CHUNK), lambda b, t: (0, 0, 0)),
            _resident((RET_HEADS, RET_CHUNK, RET_DV), lambda b, t: (0, 0, 0)),
            pl.BlockSpec((1, RET_V), lambda b, t: (0, 0)),
        ],
        out_specs=tok(RET_V),
        out_shape=jax.ShapeDtypeStruct((n, RET_V), BF16),
        scratch_shapes=[pltpu.VMEM((RET_HEADS, RET_DK, RET_DV), F32),
                        staged(RET_QK, BF16), staged(RET_QK, BF16),
                        staged(RET_QK, BF16), staged(RET_V, BF16), staged(RET_V, F32)],
        compiler_params=_params(("parallel", "arbitrary")),
        name="retention",
    )(xn, w_in, cos, sin, zeta_tab, decay, xi, gn_gain)


def _retention_constants():
    heads = jnp.arange(RET_HEADS, dtype=F32)
    log_gamma = jnp.log1p(-jnp.exp2(-5.0 - heads))
    idx = jnp.arange(RET_CHUNK, dtype=F32)
    rel = idx[:, None] - idx[None, :]
    inner = jnp.where(rel[None] >= 0,
                      jnp.exp(jnp.maximum(rel, 0.0)[None] * log_gamma[:, None, None]), 0.0)
    xi = jnp.exp((idx + 1.0)[None, :] * log_gamma[:, None])
    zeta = jnp.exp((RET_CHUNK - 1.0 - idx)[None, :] * log_gamma[:, None])
    return inner, xi, zeta


def _chunk_decay():
    return tuple(float((1.0 - 2.0 ** (-5 - hd)) ** RET_CHUNK) for hd in range(RET_HEADS))


def _stick_breaking_kernel(xn_ref, wq_ref, wk_ref, wv_ref, o_ref,
                           q_ref, k_ref, v_ref, qnew_ref, knew_ref, vnew_ref,
                           acc_ref, rest_ref):
    blk = SB_BLOCK
    seq = q_ref.shape[0]

    projections = ((wq_ref, q_ref, qnew_ref), (wk_ref, k_ref, knew_ref),
                   (wv_ref, v_ref, vnew_ref))

    ahead_rows = qnew_ref.shape[0]

    def project(r0):
        x = xn_ref[pl.ds(r0, ahead_rows), :]
        for w_ref, _, new_ref in projections:
            new_ref[...] = _dot(x, w_ref[...]).astype(BF16)

    def commit(r0):
        for _, dst, new_ref in projections:
            dst[pl.ds(r0, ahead_rows), :] = new_ref[...]

    scale = SB_DH ** -0.5
    row = lax.broadcasted_iota(jnp.int32, (blk, blk), 0)
    col = lax.broadcasted_iota(jnp.int32, (blk, blk), 1)
    causal = col < row
    r2 = lax.broadcasted_iota(jnp.int32, (2 * blk, 2 * blk), 0) & (blk - 1)
    c2 = lax.broadcasted_iota(jnp.int32, (2 * blk, 2 * blk), 1)
    neg_suffix_ones = jnp.where((r2 >= c2) | (c2 >= blk), -1.0, 0.0).astype(BF16)
    contract_last = (((1,), (1,)), ((), ()))
    heads = range(SB_GROUP)
    cols = [slice(g * SB_DH, (g + 1) * SB_DH) for g in heads]

    def tiles(qblocks, nblk, diagonal, side_work=None):
        width = nblk * blk
        sub = [slice(c * blk, (c + 1) * blk) for c in range(nblk)]
        chains = [(slot, g, q0, k0) for slot, q0, k0 in qblocks for g in heads]
        count = len(chains)
        raws, zs, sums, worst = [None] * count, [None] * count, [None] * count, {}

        def scores(i):
            _, g, q0, k0 = chains[i]
            raws[i] = lax.dot_general(q_ref[pl.ds(q0, blk), cols[g]],
                                      k_ref[pl.ds(k0, width), cols[g]],
                                      contract_last, preferred_element_type=F32)

        def suffix_sums(i):
            zs[i] = raws[i] * scale
            decayed = jnp.exp2(jnp.abs(raws[i]) * (-scale * LOG2_E))
            softplus = jnp.maximum(zs[i], 0.0) + jnp.log(1.0 + decayed)
            pieces = [softplus[:, s] for s in sub]
            if diagonal:
                pieces[-1] = jnp.where(causal, pieces[-1], 0.0)
            split = []
            for piece in pieces:
                hi = piece.astype(BF16)
                split.append(jnp.concatenate(
                    [hi, (piece - hi.astype(F32)).astype(BF16)], axis=1))
            both = _dot(jnp.concatenate(split, axis=0), neg_suffix_ones)
            sums[i] = [both[s] for s in sub]

        def weigh(i):
            slot, g, _, k0 = chains[i]
            rest = None if diagonal else rest_ref[slot, g]
            weights = [None] * nblk
            for c in reversed(range(nblk)):
                logw = zs[i][:, sub[c]] + sums[i][c][:, :blk]
                if rest is not None:
                    logw = logw + rest
                a = jnp.exp(logw)
                if diagonal and c == nblk - 1:
                    a = jnp.where(causal, a, 0.0)
                weights[c] = a.astype(BF16)
                row_sum = sums[i][c][:, blk:]
                rest = row_sum if rest is None else rest + row_sum
            part = _dot(jnp.concatenate(weights, axis=1), v_ref[pl.ds(k0, width), cols[g]])
            if diagonal:
                acc_ref[slot, g] = part
            else:
                acc_ref[slot, g] += part
            rest_ref[slot, g] = rest
            worst[slot] = rest if slot not in worst else jnp.maximum(worst[slot], rest)

        for i in range(count):
            scores(i)
        if side_work is not None:
            side_work()
        for stage in (suffix_sums, weigh):
            for i in range(count):
                stage(i)
        return [jnp.max(worst[slot]) for slot, _, _ in qblocks]

    def finish(slot, q0, j, worst):
        def more(c):
            j, worst = c
            return jnp.logical_and(j >= 0, worst > SB_SKIP_LOG)

        def step(c):
            j, _ = c
            return j - 1, tiles([(slot, q0, pl.multiple_of(j * blk, blk))], 1, False)[0]

        lax.while_loop(more, step, (j, worst))
        for g in heads:
            o_ref[pl.ds(q0, blk), cols[g]] = acc_ref[slot, g].astype(BF16)

    nq = seq // blk
    reach = SB_FIRST_PASS - 1
    prefix = reach + (nq - reach) % SB_QBLOCKS
    pending = list(range(0, (prefix + SB_QBLOCKS) * blk, ahead_rows))
    for qi in range(prefix):
        while pending and pending[0] < (qi + 1) * blk:
            project(pending[0])
            commit(pending.pop(0))
        beside = pending.pop(0) if pending else None
        tiles([(0, qi * blk, 0)], qi + 1, True,
              side_work=None if beside is None else functools.partial(project, beside))
        if beside is not None:
            commit(beside)
        for g in heads:
            o_ref[qi * blk:(qi + 1) * blk, cols[g]] = acc_ref[0, g].astype(BF16)
    for r0 in pending:
        project(r0)
        commit(r0)

    def q_blocks(step, carry):
        first = prefix + step * SB_QBLOCKS
        qblocks = [(slot, pl.multiple_of((first + slot) * blk, blk),
                    pl.multiple_of((first + slot - reach) * blk, blk))
                   for slot in range(SB_QBLOCKS)]
        ahead = pl.multiple_of(
            jnp.minimum(first + SB_QBLOCKS, nq - SB_QBLOCKS) * blk, blk)
        worst = tiles(qblocks, SB_FIRST_PASS, True,
                      side_work=functools.partial(project, ahead))
        for slot, q0, _ in qblocks:
            finish(slot, q0, first + slot - SB_FIRST_PASS, worst[slot])
        commit(ahead)
        return carry

    lax.fori_loop(0, (nq - prefix) // SB_QBLOCKS, q_blocks, 0)


def _stick_breaking(xn, w_in, batch, seq):
    xn = xn.reshape(batch, seq, D_MODEL)
    groups = SB_HEADS // SB_GROUP
    width = SB_GROUP * SB_DH
    weight = lambda part: pl.BlockSpec((D_MODEL, width), lambda b, g: (0, part * groups + g))
    staged = pltpu.VMEM((seq, width), BF16)
    ahead = pltpu.VMEM((SB_QBLOCKS * SB_BLOCK, width), BF16)
    out = pl.pallas_call(
        _stick_breaking_kernel,
        grid=(batch, groups),
        in_specs=[pl.BlockSpec((None, seq, D_MODEL), lambda b, g: (b, 0, 0)),
                  weight(0), weight(1), weight(2)],
        out_specs=pl.BlockSpec((None, seq, width), lambda b, g: (b, 0, g)),
        out_shape=jax.ShapeDtypeStruct((batch, seq, SB_WIDTH), BF16),
        scratch_shapes=[staged, staged, staged, ahead, ahead, ahead,
                        pltpu.VMEM((SB_QBLOCKS, SB_GROUP, SB_BLOCK, SB_DH), F32),
                        pltpu.VMEM((SB_QBLOCKS, SB_GROUP, SB_BLOCK, SB_BLOCK), F32)],
        compiler_params=_params(("parallel", "arbitrary")),
        name="stick_breaking",
    )(xn, w_in, w_in, w_in)
    return out.reshape(batch * seq, SB_WIDTH)


def kernel(x, p, positions, norm_gains, ffn_w_gate, ffn_w_up, ffn_w_down, ret_w_in,
           ret_gn_gain, ret_w_out, sb_w_in, sb_w_out, ple_w_gate, ple_w_proj):
    batch, seq, _ = x.shape
    n = batch * seq
    h = x.reshape(n, D_MODEL)
    p = p.reshape(DEPTH, n, PLE_DIM)
    gains = norm_gains
    ffn_stacks = (ffn_w_gate, ffn_w_up, ffn_w_down)
    cos, sin, ffn_weights = _rope_tables(positions, [(w, (0, 0)) for w in ffn_stacks])
    inner, xi, zeta = _retention_constants()
    xi_tab = jnp.broadcast_to(xi[:, :, None], (RET_HEADS, RET_CHUNK, RET_DV))
    zeta_tab = jnp.tile(jnp.repeat(zeta.T, RET_DK, axis=1), (RET_STEP // RET_CHUNK, 1))
    chunk_decay = _chunk_decay()

    for layer in range(DEPTH):
        j = layer // N_MIXERS
        retention = layer % N_MIXERS == 0
        mixer_stacks = (ret_w_in, ret_w_out) if retention else (sb_w_in, sb_w_out)
        h, xn, cast = _ffn(h, gains, layer, ffn_weights,
                           [(w, (j,)) for w in mixer_stacks]
                           + [(w, (layer, 1)) for w in ffn_stacks]
                           + [(ple_w_gate, (layer,)), (ple_w_proj, (layer,))])
        w_in, w_out, ffn_weights, ple_weights = cast[0], cast[1], cast[2:5], cast[5:7]
        if retention:
            mixed = _retention(xn, w_in, cos, sin, zeta_tab, inner, xi_tab,
                               ret_gn_gain[j].reshape(1, RET_V), chunk_decay, batch, seq)
        else:
            mixed = _stick_breaking(xn, w_in, batch, seq)
        next_ffn = [(w, (layer + 1, 0)) for w in ffn_stacks] if layer + 1 < DEPTH else []
        h, ffn_weights = _tail(h, mixed, p, gains, layer, w_out, ffn_weights, ple_weights,
                               next_ffn)
    return h.reshape(batch, seq, D_MODEL)
```

```python
import functools

import jax
import jax.numpy as jnp
from jax import lax
from jax.experimental import pallas as pl
from jax.experimental.pallas import tpu as pltpu

F32 = jnp.float32
BF16 = jnp.bfloat16

D_MODEL = 1024
DEPTH = 4
N_MIXERS = 2
PLE_DIM = 256
D_FF = 2816
FFN_RES_WEIGHT = 0.5
RET_HEADS = 4
RET_DK = D_MODEL // RET_HEADS
RET_QK = RET_HEADS * RET_DK
RET_DV = 2 * RET_DK
RET_V = RET_HEADS * RET_DV
RET_CHUNK = 256
ROPE_BASE = 10000.0
ROPE_HALF = RET_DK // 2
GN_EPS = 1e-5
SB_HEADS = 8
SB_DH = D_MODEL // SB_HEADS
SB_WIDTH = SB_HEADS * SB_DH
SB_BLOCK = 128
N_NORMS = 8
RMS_EPS = 1e-6
LOG2_E = 1.4426950408889634

VMEM_LIMIT_BYTES = 56 * 1024 * 1024

MXU_WIDTH = 256
FFN_TM = 512
ROPE_TM = 1024
RET_STEP = 1024
FFN_TF = MXU_WIDTH
FFN_PART_WEIGHTS = (1, 1)
SB_GROUP = 4
SB_FIRST_PASS = 3
SB_QBLOCKS = 2

SB_SKIP_LOG = -105.0


def _params(sem):
    return pltpu.CompilerParams(dimension_semantics=sem,
                                vmem_limit_bytes=VMEM_LIMIT_BYTES)


def _resident(shape, index_map):
    return pl.BlockSpec(shape, index_map, pipeline_mode=pl.Buffered(1))


def _rms(x, g):
    ms = jnp.mean(x * x, axis=-1, keepdims=True)
    return x * lax.rsqrt(ms + RMS_EPS) * g


def _dot(a, b):
    return jnp.dot(a, b, preferred_element_type=F32)


def _silu(x):
    return x * jax.nn.sigmoid(x)


def _gain(gains_ref, k):
    return gains_ref[k:k + 1, :]


def _row_parts(rows):
    bounds = [0] + [rows * sum(FFN_PART_WEIGHTS[:i + 1]) // sum(FFN_PART_WEIGHTS)
                    for i in range(len(FFN_PART_WEIGHTS))]
    return [slice(lo, hi) for lo, hi in zip(bounds[:-1], bounds[1:])]


def _half_ffn(parts, read_h, add_step, g_pre, g_post, wg_ref, wu_ref, wd_ref,
              xn_ref, hid_ref, side_work):
    def hidden(part, c):
        cols = slice(c * FFN_TF, (c + 1) * FFN_TF)
        xn = xn_ref[part, :]
        hid_ref[part, cols] = (_silu(_dot(xn, wg_ref[:, cols]))
                               * _dot(xn, wu_ref[:, cols])).astype(BF16)

    for i, part in enumerate(parts):
        xn_ref[part, :] = _rms(read_h(part), g_pre).astype(BF16)
        for c in range(D_FF // FFN_TF):
            hidden(part, c)
        if i == 0:
            side_work()
    g_step = g_post * FFN_RES_WEIGHT
    fs = [_dot(hid_ref[part, :], wd_ref[...]) for part in parts]
    for part, f in zip(parts, fs):
        add_step(part, _rms(f, g_step))


class _CastJobs:
    def __init__(self, jobs, steps):
        self.operands, self.in_specs, self.out_specs, self.out_shapes, self.shapes = [], [], [], [], []
        for w, lead in jobs:
            rows, cols = w.shape[-2:]
            slab = rows // steps
            assert slab * steps == rows and len(lead) == w.ndim - 2
            self.operands.append(w.reshape(w.shape[:-2] + (steps, slab, cols)))
            self.in_specs.append(pl.BlockSpec(
                (None,) * (len(lead) + 1) + (slab, cols),
                functools.partial(lambda lead, i: lead + (i, 0, 0), tuple(lead))))
            self.out_specs.append(pl.BlockSpec((None, slab, cols), lambda i: (i, 0, 0)))
            self.out_shapes.append(jax.ShapeDtypeStruct((steps, slab, cols), BF16))
            self.shapes.append((rows, cols))

    def __len__(self):
        return len(self.operands)

    def matrices(self, outputs):
        return [o.reshape(shape) for o, shape in zip(outputs, self.shapes)]


def _cast_slabs(src_refs, dst_refs):
    for src, dst in zip(src_refs, dst_refs):
        dst[...] = src[...].astype(BF16)


def _whole(w):
    return _resident(w.shape, lambda i: (0, 0))


def _ffn_kernel(n_cast, h_ref, gains_ref, wg_ref, wu_ref, wd_ref, *refs):
    cast_src, (o_ref, normed_ref, *cast_dst), (xn_ref, hid_ref) = (
        refs[:n_cast], refs[n_cast:2 * n_cast + 2], refs[2 * n_cast + 2:])

    def add_step(part, step):
        h = h_ref[part, :] + step
        o_ref[part, :] = h
        normed_ref[part, :] = _rms(h, _gain(gains_ref, 2)).astype(BF16)

    _half_ffn(_row_parts(h_ref.shape[0]), lambda part: h_ref[part, :], add_step,
              _gain(gains_ref, 0), _gain(gains_ref, 1), wg_ref, wu_ref, wd_ref,
              xn_ref, hid_ref, functools.partial(_cast_slabs, cast_src, cast_dst))


def _ffn_scratch():
    return [pltpu.VMEM((FFN_TM, D_MODEL), BF16), pltpu.VMEM((FFN_TM, D_FF), BF16)]


def _ffn(h, gains, layer, ffn_weights, cast_jobs):
    n = h.shape[0]
    steps = n // FFN_TM
    casts = _CastJobs(cast_jobs, steps)
    tile = pl.BlockSpec((FFN_TM, D_MODEL), lambda i: (i, 0))
    outs = pl.pallas_call(
        functools.partial(_ffn_kernel, len(casts)),
        grid=(steps,),
        in_specs=[
            tile,
            pl.BlockSpec((None, N_NORMS, D_MODEL), lambda i: (layer, 0, 0)),
        ] + [_whole(w) for w in ffn_weights] + casts.in_specs,
        out_specs=[tile, tile] + casts.out_specs,
        out_shape=[jax.ShapeDtypeStruct(h.shape, F32),
                   jax.ShapeDtypeStruct(h.shape, BF16)] + casts.out_shapes,
        scratch_shapes=_ffn_scratch(),
        compiler_params=_params(("parallel",)),
        name="ffn",
    )(h, gains, *ffn_weights, *casts.operands)
    return outs[0], outs[1], casts.matrices(outs[2:])


def _tail_kernel(n_cast, h_ref, mixed_ref, p_ref, gains_ref, wout_ref, wg_ref, wu_ref,
                 wd_ref, wgate_ref, wproj_ref, *refs):
    cast_src, (o_ref, *cast_dst), (xn_ref, hid_ref, h_scratch) = (
        refs[:n_cast], refs[n_cast:2 * n_cast + 1], refs[2 * n_cast + 1:])
    gain = functools.partial(_gain, gains_ref)
    parts = _row_parts(h_ref.shape[0])
    mixer_out = {part.start: _dot(mixed_ref[part, :], wout_ref[...]) for part in parts}

    def after_mixer(part):
        h = h_ref[part, :] + _rms(mixer_out[part.start], gain(3))
        h_scratch[part, :] = h
        return h

    def add_step_and_embedding(part, step):
        h = h_scratch[part, :] + step
        gate = jax.nn.sigmoid(_dot(_rms(h, gain(6)).astype(BF16), wgate_ref[...]))
        e = _dot(p_ref[part, :].astype(BF16), wproj_ref[...])
        o_ref[part, :] = h + _rms(gate * e, gain(7))

    _half_ffn(parts, after_mixer, add_step_and_embedding, gain(4), gain(5),
              wg_ref, wu_ref, wd_ref, xn_ref, hid_ref,
              functools.partial(_cast_slabs, cast_src, cast_dst))


def _tail(h, mixed, p, gains, layer, w_out, ffn_weights, ple_weights, cast_jobs):
    n = h.shape[0]
    steps = n // FFN_TM
    casts = _CastJobs(cast_jobs, steps)
    weights = [w_out, *ffn_weights, *ple_weights]
    outs = pl.pallas_call(
        functools.partial(_tail_kernel, len(casts)),
        grid=(steps,),
        in_specs=[
            pl.BlockSpec((FFN_TM, D_MODEL), lambda i: (i, 0)),
            pl.BlockSpec((FFN_TM, mixed.shape[1]), lambda i: (i, 0)),
            pl.BlockSpec((None, FFN_TM, PLE_DIM), lambda i: (layer, i, 0)),
            pl.BlockSpec((None, N_NORMS, D_MODEL), lambda i: (layer, 0, 0)),
        ] + [_whole(w) for w in weights] + casts.in_specs,
        out_specs=[pl.BlockSpec((FFN_TM, D_MODEL), lambda i: (i, 0))] + casts.out_specs,
        out_shape=[jax.ShapeDtypeStruct(h.shape, F32)] + casts.out_shapes,
        scratch_shapes=_ffn_scratch() + [pltpu.VMEM((FFN_TM, D_MODEL), F32)],
        compiler_params=_params(("parallel",)),
        name="tail",
    )(h, mixed, p, gains, *weights, *casts.operands)
    return outs[0], casts.matrices(outs[1:])


def _rope_table_kernel(n_cast, pos_ref, inv_ref, *refs):
    cast_src, (cos_ref, sin_ref, *cast_dst) = refs[:n_cast], refs[n_cast:]
    ang = pos_ref[...].astype(F32) * inv_ref[...]
    cos_ref[...] = jnp.cos(ang)
    sin_ref[...] = jnp.sin(ang)
    _cast_slabs(cast_src, cast_dst)


def _rope_tables(positions, cast_jobs):
    n = positions.size
    tm = ROPE_TM
    casts = _CastJobs(cast_jobs, n // tm)
    inv = ROPE_BASE ** (-jnp.arange(ROPE_HALF, dtype=F32) / ROPE_HALF)
    out = jax.ShapeDtypeStruct((n, ROPE_HALF), F32)
    outs = pl.pallas_call(
        functools.partial(_rope_table_kernel, len(casts)),
        grid=(n // tm,),
        in_specs=[pl.BlockSpec((tm, 1), lambda i: (i, 0)),
                  pl.BlockSpec((1, ROPE_HALF), lambda i: (0, 0))] + casts.in_specs,
        out_specs=[pl.BlockSpec((tm, ROPE_HALF), lambda i: (i, 0))] * 2 + casts.out_specs,
        out_shape=[out, out] + casts.out_shapes,
        compiler_params=_params(("parallel",)),
        name="rope_tables",
    )(positions.reshape(n, 1), inv.reshape(1, ROPE_HALF), *casts.operands)
    return outs[0], outs[1], casts.matrices(outs[2:])


def _retention_kernel(chunk_decay, xn_ref, w_ref, cos_ref, sin_ref, zeta_ref,
                      decay_ref, xi_ref, gn_gain_ref, o_ref,
                      state_ref, q_ref, k_ref, kz_ref, v_ref, gate_ref):
    @pl.when(pl.program_id(1) == 0)
    def _():
        state_ref[...] = jnp.zeros_like(state_ref)

    k_scale = RET_DK ** -0.5
    contract_last = (((1,), (1,)), ((), ()))
    contract_first = (((0,), (0,)), ((), ()))

    def projection_steps(hd):
        lo = hd * RET_DK
        mid = lo + ROPE_HALF
        hi = lo + RET_DK

        def rotated(col):
            t = _dot(xn_ref[...], w_ref[:, col:col + RET_DK])
            t1, t2 = t[:, :ROPE_HALF], t[:, ROPE_HALF:]
            cos = cos_ref[...]
            sin = sin_ref[...]
            return t1 * cos - t2 * sin, t1 * sin + t2 * cos

        def queries():
            q1, q2 = rotated(lo)
            q_ref[:, lo:mid] = q1.astype(BF16)
            q_ref[:, mid:hi] = q2.astype(BF16)

        def keys():
            k1, k2 = rotated(RET_QK + lo)
            k1 = k1 * k_scale
            k2 = k2 * k_scale
            k_ref[:, lo:mid] = k1.astype(BF16)
            k_ref[:, mid:hi] = k2.astype(BF16)
            kz_ref[:, lo:mid] = (k1 * zeta_ref[:, lo:mid]).astype(BF16)
            kz_ref[:, mid:hi] = (k2 * zeta_ref[:, mid:hi]).astype(BF16)

        def values(c):
            cols = slice(hd * RET_DV + c * MXU_WIDTH, hd * RET_DV + (c + 1) * MXU_WIDTH)
            v0 = 2 * RET_QK + cols.start
            v_ref[:, cols] = _dot(xn_ref[...], w_ref[:, v0:v0 + MXU_WIDTH]).astype(BF16)

        def gates(c):
            cols = slice(hd * RET_DV + c * MXU_WIDTH, hd * RET_DV + (c + 1) * MXU_WIDTH)
            g0 = 2 * RET_QK + RET_V + cols.start
            gate_ref[:, cols] = _dot(xn_ref[...], w_ref[:, g0:g0 + MXU_WIDTH])

        halves = range(RET_DV // MXU_WIDTH)
        return ([queries, keys] + [functools.partial(values, c) for c in halves]
                + [functools.partial(gates, c) for c in halves])

    def recurrence_steps(hd):
        return [functools.partial(recur, hd, c) for c in range(RET_STEP // RET_CHUNK)]

    def recur(hd, c):
        qk_cols = slice(hd * RET_DK, (hd + 1) * RET_DK)
        v_cols = slice(hd * RET_DV, (hd + 1) * RET_DV)
        rows = slice(c * RET_CHUNK, (c + 1) * RET_CHUNK)
        q = q_ref[rows, qk_cols]
        v = v_ref[rows, v_cols]
        scores = lax.dot_general(q, k_ref[rows, qk_cols], contract_last,
                                 preferred_element_type=F32) * decay_ref[hd]
        state = state_ref[hd]
        o = _dot(scores.astype(BF16), v) + _dot(q, state.astype(BF16)) * xi_ref[hd]
        state_ref[hd] = state * chunk_decay[hd] + lax.dot_general(
            kz_ref[rows, qk_cols], v, contract_first, preferred_element_type=F32)
        mu = jnp.mean(o, axis=-1, keepdims=True)
        d = o - mu
        var = jnp.mean(d * d, axis=-1, keepdims=True)
        normed = d * lax.rsqrt(var + GN_EPS) * gn_gain_ref[:, v_cols]
        o_ref[rows, v_cols] = (_silu(gate_ref[rows, v_cols]) * normed).astype(BF16)

    for step in projection_steps(0):
        step()
    for hd in range(RET_HEADS):
        ahead = projection_steps(hd + 1) if hd + 1 < RET_HEADS else []
        chunks = recurrence_steps(hd)
        for i in range(max(len(ahead), len(chunks))):
            for steps in (ahead, chunks):
                if i < len(steps):
                    steps[i]()


def _retention(xn, w_in, cos, sin, zeta_tab, decay, xi, gn_gain, chunk_decay, batch, seq):
    n = xn.shape[0]
    steps = seq // RET_STEP
    tok = lambda width: pl.BlockSpec((RET_STEP, width), lambda b, t: (b * steps + t, 0))
    staged = lambda width, dtype: pltpu.VMEM((RET_STEP, width), dtype)
    return pl.pallas_call(
        functools.partial(_retention_kernel, chunk_decay),
        grid=(batch, steps),
        in_specs=[
            tok(D_MODEL),
            _resident(w_in.shape, lambda b, t: (0, 0)),
            tok(ROPE_HALF), tok(ROPE_HALF),
            _resident((RET_STEP, RET_QK), lambda b, t: (0, 0)),
            _resident((RET_HEADS, RET_CHUNK, RET_CHUNK), lambda b, t: (0, 0, 0)),
            _resident((RET_HEADS, RET_CHUNK, RET_DV), lambda b, t: (0, 0, 0)),
            pl.BlockSpec((1, RET_V), lambda b, t: (0, 0)),
        ],
        out_specs=tok(RET_V),
        out_shape=jax.ShapeDtypeStruct((n, RET_V), BF16),
        scratch_shapes=[pltpu.VMEM((RET_HEADS, RET_DK, RET_DV), F32),
                        staged(RET_QK, BF16), staged(RET_QK, BF16),
                        staged(RET_QK, BF16), staged(RET_V, BF16), staged(RET_V, F32)],
        compiler_params=_params(("parallel", "arbitrary")),
        name="retention",
    )(xn, w_in, cos, sin, zeta_tab, decay, xi, gn_gain)


def _retention_constants():
    heads = jnp.arange(RET_HEADS, dtype=F32)
    log_gamma = jnp.log1p(-jnp.exp2(-5.0 - heads))
    idx = jnp.arange(RET_CHUNK, dtype=F32)
    rel = idx[:, None] - idx[None, :]
    inner = jnp.where(rel[None] >= 0,
                      jnp.exp(jnp.maximum(rel, 0.0)[None] * log_gamma[:, None, None]), 0.0)
    xi = jnp.exp((idx + 1.0)[None, :] * log_gamma[:, None])
    zeta = jnp.exp((RET_CHUNK - 1.0 - idx)[None, :] * log_gamma[:, None])
    return inner, xi, zeta


def _chunk_decay():
    return tuple(float((1.0 - 2.0 ** (-5 - hd)) ** RET_CHUNK) for hd in range(RET_HEADS))


def _stick_breaking_kernel(xn_ref, wq_ref, wk_ref, wv_ref, o_ref,
                           q_ref, k_ref, v_ref, qnew_ref, knew_ref, vnew_ref,
                           acc_ref, rest_ref):
    blk = SB_BLOCK
    seq = q_ref.shape[0]

    projections = ((wq_ref, q_ref, qnew_ref), (wk_ref, k_ref, knew_ref),
                   (wv_ref, v_ref, vnew_ref))

    ahead_rows = qnew_ref.shape[0]

    def project(r0):
        x = xn_ref[pl.ds(r0, ahead_rows), :]
        for w_ref, _, new_ref in projections:
            new_ref[...] = _dot(x, w_ref[...]).astype(BF16)

    def commit(r0):
        for _, dst, new_ref in projections:
            dst[pl.ds(r0, ahead_rows), :] = new_ref[...]

    scale = SB_DH ** -0.5
    row = lax.broadcasted_iota(jnp.int32, (blk, blk), 0)
    col = lax.broadcasted_iota(jnp.int32, (blk, blk), 1)
    causal = col < row
    r2 = lax.broadcasted_iota(jnp.int32, (2 * blk, 2 * blk), 0) & (blk - 1)
    c2 = lax.broadcasted_iota(jnp.int32, (2 * blk, 2 * blk), 1)
    neg_suffix_ones = jnp.where((r2 >= c2) | (c2 >= blk), -1.0, 0.0).astype(BF16)
    contract_last = (((1,), (1,)), ((), ()))
    heads = range(SB_GROUP)
    cols = [slice(g * SB_DH, (g + 1) * SB_DH) for g in heads]

    def tiles(qblocks, nblk, diagonal, side_work=None):
        width = nblk * blk
        sub = [slice(c * blk, (c + 1) * blk) for c in range(nblk)]
        chains = [(slot, g, q0, k0) for slot, q0, k0 in qblocks for g in heads]
        count = len(chains)
        raws, zs, sums, worst = [None] * count, [None] * count, [None] * count, {}

        def scores(i):
            _, g, q0, k0 = chains[i]
            raws[i] = lax.dot_general(q_ref[pl.ds(q0, blk), cols[g]],
                                      k_ref[pl.ds(k0, width), cols[g]],
                                      contract_last, preferred_element_type=F32)

        def suffix_sums(i):
            zs[i] = raws[i] * scale
            decayed = jnp.exp2(jnp.abs(raws[i]) * (-scale * LOG2_E))
            softplus = jnp.maximum(zs[i], 0.0) + jnp.log(1.0 + decayed)
            pieces = [softplus[:, s] for s in sub]
            if diagonal:
                pieces[-1] = jnp.where(causal, pieces[-1], 0.0)
            split = []
            for piece in pieces:
                hi = piece.astype(BF16)
                split.append(jnp.concatenate(
                    [hi, (piece - hi.astype(F32)).astype(BF16)], axis=1))
            both = _dot(jnp.concatenate(split, axis=0), neg_suffix_ones)
            sums[i] = [both[s] for s in sub]

        def weigh(i):
            slot, g, _, k0 = chains[i]
            rest = None if diagonal else rest_ref[slot, g]
            weights = [None] * nblk
            for c in reversed(range(nblk)):
                logw = zs[i][:, sub[c]] + sums[i][c][:, :blk]
                if rest is not None:
                    logw = logw + rest
                a = jnp.exp(logw)
                if diagonal and c == nblk - 1:
                    a = jnp.where(causal, a, 0.0)
                weights[c] = a.astype(BF16)
                row_sum = sums[i][c][:, blk:]
                rest = row_sum if rest is None else rest + row_sum
            part = _dot(jnp.concatenate(weights, axis=1), v_ref[pl.ds(k0, width), cols[g]])
            if diagonal:
                acc_ref[slot, g] = part
            else:
                acc_ref[slot, g] += part
            rest_ref[slot, g] = rest
            worst[slot] = rest if slot not in worst else jnp.maximum(worst[slot], rest)

        for i in range(count):
            scores(i)
        if side_work is not None:
            side_work()
        for stage in (suffix_sums, weigh):
            for i in range(count):
                stage(i)
        return [jnp.max(worst[slot]) for slot, _, _ in qblocks]

    def finish(slot, q0, j, worst):
        def more(c):
            j, worst = c
            return jnp.logical_and(j >= 0, worst > SB_SKIP_LOG)

        def step(c):
            j, _ = c
            return j - 1, tiles([(slot, q0, pl.multiple_of(j * blk, blk))], 1, False)[0]

        lax.while_loop(more, step, (j, worst))
        for g in heads:
            o_ref[pl.ds(q0, blk), cols[g]] = acc_ref[slot, g].astype(BF16)

    nq = seq // blk
    reach = SB_FIRST_PASS - 1
    prefix = reach + (nq - reach) % SB_QBLOCKS
    pending = list(range(0, (prefix + SB_QBLOCKS) * blk, ahead_rows))
    for qi in range(prefix):
        while pending and pending[0] < (qi + 1) * blk:
            project(pending[0])
            commit(pending.pop(0))
        beside = pending.pop(0) if pending else None
        tiles([(0, qi * blk, 0)], qi + 1, True,
              side_work=None if beside is None else functools.partial(project, beside))
        if beside is not None:
            commit(beside)
        for g in heads:
            o_ref[qi * blk:(qi + 1) * blk, cols[g]] = acc_ref[0, g].astype(BF16)
    for r0 in pending:
        project(r0)
        commit(r0)

    def q_blocks(step, carry):
        first = prefix + step * SB_QBLOCKS
        qblocks = [(slot, pl.multiple_of((first + slot) * blk, blk),
                    pl.multiple_of((first + slot - reach) * blk, blk))
                   for slot in range(SB_QBLOCKS)]
        ahead = pl.multiple_of(
            jnp.minimum(first + SB_QBLOCKS, nq - SB_QBLOCKS) * blk, blk)
        worst = tiles(qblocks, SB_FIRST_PASS, True,
                      side_work=functools.partial(project, ahead))
        for slot, q0, _ in qblocks:
            finish(slot, q0, first + slot - SB_FIRST_PASS, worst[slot])
        commit(ahead)
        return carry

    lax.fori_loop(0, (nq - prefix) // SB_QBLOCKS, q_blocks, 0)


def _stick_breaking(xn, w_in, batch, seq):
    xn = xn.reshape(batch, seq, D_MODEL)
    groups = SB_HEADS // SB_GROUP
    width = SB_GROUP * SB_DH
    weight = lambda part: pl.BlockSpec((D_MODEL, width), lambda b, g: (0, part * groups + g))
    staged = pltpu.VMEM((seq, width), BF16)
    ahead = pltpu.VMEM((SB_QBLOCKS * SB_BLOCK, width), BF16)
    out = pl.pallas_call(
        _stick_breaking_kernel,
        grid=(batch, groups),
        in_specs=[pl.BlockSpec((None, seq, D_MODEL), lambda b, g: (b, 0, 0)),
                  weight(0), weight(1), weight(2)],
        out_specs=pl.BlockSpec((None, seq, width), lambda b, g: (b, 0, g)),
        out_shape=jax.ShapeDtypeStruct((batch, seq, SB_WIDTH), BF16),
        scratch_shapes=[staged, staged, staged, ahead, ahead, ahead,
                        pltpu.VMEM((SB_QBLOCKS, SB_GROUP, SB_BLOCK, SB_DH), F32),
                        pltpu.VMEM((SB_QBLOCKS, SB_GROUP, SB_BLOCK, SB_BLOCK), F32)],
        compiler_params=_params(("parallel", "arbitrary")),
        name="stick_breaking",
    )(xn, w_in, w_in, w_in)
    return out.reshape(batch * seq, SB_WIDTH)


def kernel(x, p, positions, norm_gains, ffn_w_gate, ffn_w_up, ffn_w_down, ret_w_in,
           ret_gn_gain, ret_w_out, sb_w_in, sb_w_out, ple_w_gate, ple_w_proj):
    batch, seq, width = x.shape
    n = batch * seq
    assert width == D_MODEL and p.shape == (DEPTH, batch, seq, PLE_DIM)
    assert n % FFN_TM == 0 and n % ROPE_TM == 0 and seq % RET_STEP == 0
    assert RET_STEP % RET_CHUNK == 0 and seq % (SB_BLOCK * SB_QBLOCKS) == 0
    h = x.reshape(n, D_MODEL)
    p = p.reshape(DEPTH, n, PLE_DIM)
    gains = norm_gains
    ffn_stacks = (ffn_w_gate, ffn_w_up, ffn_w_down)
    cos, sin, ffn_weights = _rope_tables(positions, [(w, (0, 0)) for w in ffn_stacks])
    inner, xi, zeta = _retention_constants()
    xi_tab = jnp.broadcast_to(xi[:, :, None], (RET_HEADS, RET_CHUNK, RET_DV))
    zeta_tab = jnp.tile(jnp.repeat(zeta.T, RET_DK, axis=1), (RET_STEP // RET_CHUNK, 1))
    chunk_decay = _chunk_decay()

    for layer in range(DEPTH):
        j = layer // N_MIXERS
        retention = layer % N_MIXERS == 0
        mixer_stacks = (ret_w_in, ret_w_out) if retention else (sb_w_in, sb_w_out)
        h, xn, cast = _ffn(h, gains, layer, ffn_weights,
                           [(w, (j,)) for w in mixer_stacks]
                           + [(w, (layer, 1)) for w in ffn_stacks]
                           + [(ple_w_gate, (layer,)), (ple_w_proj, (layer,))])
        w_in, w_out, ffn_weights, ple_weights = cast[0], cast[1], cast[2:5], cast[5:7]
        if retention:
            mixed = _retention(xn, w_in, cos, sin, zeta_tab, inner, xi_tab,
                               ret_gn_gain[j].reshape(1, RET_V), chunk_decay, batch, seq)
        else:
            mixed = _stick_breaking(xn, w_in, batch, seq)
        next_ffn = [(w, (layer + 1, 0)) for w in ffn_stacks] if layer + 1 < DEPTH else []
        h, ffn_weights = _tail(h, mixed, p, gains, layer, w_out, ffn_weights, ple_weights,
                               next_ffn)
    return h.reshape(batch, seq, D_MODEL)
```

```python
import functools

import jax
import jax.numpy as jnp
from jax import lax
from jax.experimental import pallas as pl
from jax.experimental.pallas import tpu as pltpu

F32 = jnp.float32
BF16 = jnp.bfloat16

D_MODEL = 1024
DEPTH = 4
N_MIXERS = 2
PLE_DIM = 256
D_FF = 2816
FFN_RES_WEIGHT = 0.5
RET_HEADS = 4
RET_DK = D_MODEL // RET_HEADS
RET_QK = RET_HEADS * RET_DK
RET_DV = 2 * RET_DK
RET_V = RET_HEADS * RET_DV
RET_CHUNK = 256
ROPE_BASE = 10000.0
ROPE_HALF = RET_DK // 2
GN_EPS = 1e-5
SB_HEADS = 8
SB_DH = D_MODEL // SB_HEADS
SB_WIDTH = SB_HEADS * SB_DH
SB_BLOCK = 128
N_NORMS = 8
RMS_EPS = 1e-6
LOG2_E = 1.4426950408889634

VMEM_LIMIT_BYTES = 56 * 1024 * 1024

MXU_WIDTH = 256
FFN_TM = 512
ROPE_TM = 1024
RET_STEP = 512
FFN_TF = MXU_WIDTH
FFN_PART_WEIGHTS = (1, 1)
SB_GROUP = 4
SB_FIRST_PASS = 3
SB_QBLOCKS = 2

SB_SKIP_LOG = -105.0


def _params(sem):
    return pltpu.CompilerParams(dimension_semantics=sem,
                                vmem_limit_bytes=VMEM_LIMIT_BYTES)


def _resident(shape, index_map):
    return pl.BlockSpec(shape, index_map, pipeline_mode=pl.Buffered(1))


def _rms(x, g):
    ms = jnp.mean(x * x, axis=-1, keepdims=True)
    return x * lax.rsqrt(ms + RMS_EPS) * g


def _dot(a, b):
    return jnp.dot(a, b, preferred_element_type=F32)


def _silu(x):
    return x * jax.nn.sigmoid(x)


def _gain(gains_ref, k):
    return gains_ref[k:k + 1, :]


def _row_parts(rows):
    bounds = [0] + [rows * sum(FFN_PART_WEIGHTS[:i + 1]) // sum(FFN_PART_WEIGHTS)
                    for i in range(len(FFN_PART_WEIGHTS))]
    return [slice(lo, hi) for lo, hi in zip(bounds[:-1], bounds[1:])]


def _half_ffn(parts, read_h, add_step, g_pre, g_post, wg_ref, wu_ref, wd_ref,
              xn_ref, hid_ref, side_work):
    def hidden(part, c):
        cols = slice(c * FFN_TF, (c + 1) * FFN_TF)
        xn = xn_ref[part, :]
        hid_ref[part, cols] = (_silu(_dot(xn, wg_ref[:, cols]))
                               * _dot(xn, wu_ref[:, cols])).astype(BF16)

    for i, part in enumerate(parts):
        xn_ref[part, :] = _rms(read_h(part), g_pre).astype(BF16)
        for c in range(D_FF // FFN_TF):
            hidden(part, c)
        if i == 0:
            side_work()
    g_step = g_post * FFN_RES_WEIGHT
    fs = [_dot(hid_ref[part, :], wd_ref[...]) for part in parts]
    for part, f in zip(parts, fs):
        add_step(part, _rms(f, g_step))


class _CastJobs:
    def __init__(self, jobs, steps):
        self.operands, self.in_specs, self.out_specs, self.out_shapes, self.shapes = [], [], [], [], []
        for w, lead in jobs:
            rows, cols = w.shape[-2:]
            slab = rows // steps
            assert slab * steps == rows and len(lead) == w.ndim - 2
            self.operands.append(w.reshape(w.shape[:-2] + (steps, slab, cols)))
            self.in_specs.append(pl.BlockSpec(
                (None,) * (len(lead) + 1) + (slab, cols),
                functools.partial(lambda lead, i: lead + (i, 0, 0), tuple(lead))))
            self.out_specs.append(pl.BlockSpec((None, slab, cols), lambda i: (i, 0, 0)))
            self.out_shapes.append(jax.ShapeDtypeStruct((steps, slab, cols), BF16))
            self.shapes.append((rows, cols))

    def __len__(self):
        return len(self.operands)

    def matrices(self, outputs):
        return [o.reshape(shape) for o, shape in zip(outputs, self.shapes)]


def _cast_slabs(src_refs, dst_refs):
    for src, dst in zip(src_refs, dst_refs):
        dst[...] = src[...].astype(BF16)


def _whole(w):
    return _resident(w.shape, lambda i: (0, 0))


def _ffn_kernel(n_cast, h_ref, gains_ref, wg_ref, wu_ref, wd_ref, *refs):
    cast_src, (o_ref, normed_ref, *cast_dst), (xn_ref, hid_ref) = (
        refs[:n_cast], refs[n_cast:2 * n_cast + 2], refs[2 * n_cast + 2:])

    def add_step(part, step):
        h = h_ref[part, :] + step
        o_ref[part, :] = h
        normed_ref[part, :] = _rms(h, _gain(gains_ref, 2)).astype(BF16)

    _half_ffn(_row_parts(h_ref.shape[0]), lambda part: h_ref[part, :], add_step,
              _gain(gains_ref, 0), _gain(gains_ref, 1), wg_ref, wu_ref, wd_ref,
              xn_ref, hid_ref, functools.partial(_cast_slabs, cast_src, cast_dst))


def _ffn_scratch():
    return [pltpu.VMEM((FFN_TM, D_MODEL), BF16), pltpu.VMEM((FFN_TM, D_FF), BF16)]


def _ffn(h, gains, layer, ffn_weights, cast_jobs):
    n = h.shape[0]
    steps = n // FFN_TM
    casts = _CastJobs(cast_jobs, steps)
    tile = pl.BlockSpec((FFN_TM, D_MODEL), lambda i: (i, 0))
    outs = pl.pallas_call(
        functools.partial(_ffn_kernel, len(casts)),
        grid=(steps,),
        in_specs=[
            tile,
            pl.BlockSpec((None, N_NORMS, D_MODEL), lambda i: (layer, 0, 0)),
        ] + [_whole(w) for w in ffn_weights] + casts.in_specs,
        out_specs=[tile, tile] + casts.out_specs,
        out_shape=[jax.ShapeDtypeStruct(h.shape, F32),
                   jax.ShapeDtypeStruct(h.shape, BF16)] + casts.out_shapes,
        scratch_shapes=_ffn_scratch(),
        compiler_params=_params(("parallel",)),
        name="ffn",
    )(h, gains, *ffn_weights, *casts.operands)
    return outs[0], outs[1], casts.matrices(outs[2:])


def _tail_kernel(n_cast, h_ref, mixed_ref, p_ref, gains_ref, wout_ref, wg_ref, wu_ref,
                 wd_ref, wgate_ref, wproj_ref, *refs):
    cast_src, (o_ref, *cast_dst), (xn_ref, hid_ref, h_scratch) = (
        refs[:n_cast], refs[n_cast:2 * n_cast + 1], refs[2 * n_cast + 1:])
    gain = functools.partial(_gain, gains_ref)
    parts = _row_parts(h_ref.shape[0])
    mixer_out = {part.start: _dot(mixed_ref[part, :], wout_ref[...]) for part in parts}

    def after_mixer(part):
        h = h_ref[part, :] + _rms(mixer_out[part.start], gain(3))
        h_scratch[part, :] = h
        return h

    def add_step_and_embedding(part, step):
        h = h_scratch[part, :] + step
        gate = jax.nn.sigmoid(_dot(_rms(h, gain(6)).astype(BF16), wgate_ref[...]))
        e = _dot(p_ref[part, :].astype(BF16), wproj_ref[...])
        o_ref[part, :] = h + _rms(gate * e, gain(7))

    _half_ffn(parts, after_mixer, add_step_and_embedding, gain(4), gain(5),
              wg_ref, wu_ref, wd_ref, xn_ref, hid_ref,
              functools.partial(_cast_slabs, cast_src, cast_dst))


def _tail(h, mixed, p, gains, layer, w_out, ffn_weights, ple_weights, cast_jobs):
    n = h.shape[0]
    steps = n // FFN_TM
    casts = _CastJobs(cast_jobs, steps)
    weights = [w_out, *ffn_weights, *ple_weights]
    outs = pl.pallas_call(
        functools.partial(_tail_kernel, len(casts)),
        grid=(steps,),
        in_specs=[
            pl.BlockSpec((FFN_TM, D_MODEL), lambda i: (i, 0)),
            pl.BlockSpec((FFN_TM, mixed.shape[1]), lambda i: (i, 0)),
            pl.BlockSpec((None, FFN_TM, PLE_DIM), lambda i: (layer, i, 0)),
            pl.BlockSpec((None, N_NORMS, D_MODEL), lambda i: (layer, 0, 0)),
        ] + [_whole(w) for w in weights] + casts.in_specs,
        out_specs=[pl.BlockSpec((FFN_TM, D_MODEL), lambda i: (i, 0))] + casts.out_specs,
        out_shape=[jax.ShapeDtypeStruct(h.shape, F32)] + casts.out_shapes,
        scratch_shapes=_ffn_scratch() + [pltpu.VMEM((FFN_TM, D_MODEL), F32)],
        compiler_params=_params(("parallel",)),
        name="tail",
    )(h, mixed, p, gains, *weights, *casts.operands)
    return outs[0], casts.matrices(outs[1:])


def _rope_table_kernel(n_cast, pos_ref, inv_ref, *refs):
    cast_src, (cos_ref, sin_ref, *cast_dst) = refs[:n_cast], refs[n_cast:]
    ang = pos_ref[...].astype(F32) * inv_ref[...]
    cos_ref[...] = jnp.cos(ang)
    sin_ref[...] = jnp.sin(ang)
    _cast_slabs(cast_src, cast_dst)


def _rope_tables(positions, cast_jobs):
    n = positions.size
    tm = ROPE_TM
    casts = _CastJobs(cast_jobs, n // tm)
    inv = ROPE_BASE ** (-jnp.arange(ROPE_HALF, dtype=F32) / ROPE_HALF)
    out = jax.ShapeDtypeStruct((n, ROPE_HALF), F32)
    outs = pl.pallas_call(
        functools.partial(_rope_table_kernel, len(casts)),
        grid=(n // tm,),
        in_specs=[pl.BlockSpec((tm, 1), lambda i: (i, 0)),
                  pl.BlockSpec((1, ROPE_HALF), lambda i: (0, 0))] + casts.in_specs,
        out_specs=[pl.BlockSpec((tm, ROPE_HALF), lambda i: (i, 0))] * 2 + casts.out_specs,
        out_shape=[out, out] + casts.out_shapes,
        compiler_params=_params(("parallel",)),
        name="rope_tables",
    )(positions.reshape(n, 1), inv.reshape(1, ROPE_HALF), *casts.operands)
    return outs[0], outs[1], casts.matrices(outs[2:])


def _retention_kernel(chunk_decay, xn_ref, w_ref, cos_ref, sin_ref, zeta_ref,
                      decay_ref, xi_ref, gn_gain_ref, o_ref,
                      state_ref, q_ref, k_ref, kz_ref, v_ref, gate_ref):
    @pl.when(pl.program_id(1) == 0)
    def _():
        state_ref[...] = jnp.zeros_like(state_ref)

    k_scale = RET_DK ** -0.5
    contract_last = (((1,), (1,)), ((), ()))
    contract_first = (((0,), (0,)), ((), ()))

    def projection_steps(hd):
        lo = hd * RET_DK
        mid = lo + ROPE_HALF
        hi = lo + RET_DK

        def rotated(col):
            t = _dot(xn_ref[...], w_ref[:, col:col + RET_DK])
            t1, t2 = t[:, :ROPE_HALF], t[:, ROPE_HALF:]
            cos = cos_ref[...]
            sin = sin_ref[...]
            return t1 * cos - t2 * sin, t1 * sin + t2 * cos

        def queries():
            q1, q2 = rotated(lo)
            q_ref[:, lo:mid] = q1.astype(BF16)
            q_ref[:, mid:hi] = q2.astype(BF16)

        def keys():
            k1, k2 = rotated(RET_QK + lo)
            k1 = k1 * k_scale
            k2 = k2 * k_scale
            k_ref[:, lo:mid] = k1.astype(BF16)
            k_ref[:, mid:hi] = k2.astype(BF16)
            kz_ref[:, lo:mid] = (k1 * zeta_ref[:, lo:mid]).astype(BF16)
            kz_ref[:, mid:hi] = (k2 * zeta_ref[:, mid:hi]).astype(BF16)

        def values(c):
            cols = slice(hd * RET_DV + c * MXU_WIDTH, hd * RET_DV + (c + 1) * MXU_WIDTH)
            v0 = 2 * RET_QK + cols.start
            v_ref[:, cols] = _dot(xn_ref[...], w_ref[:, v0:v0 + MXU_WIDTH]).astype(BF16)

        def gates(c):
            cols = slice(hd * RET_DV + c * MXU_WIDTH, hd * RET_DV + (c + 1) * MXU_WIDTH)
            g0 = 2 * RET_QK + RET_V + cols.start
            gate_ref[:, cols] = _dot(xn_ref[...], w_ref[:, g0:g0 + MXU_WIDTH])

        halves = range(RET_DV // MXU_WIDTH)
        return ([queries, keys] + [functools.partial(values, c) for c in halves]
                + [functools.partial(gates, c) for c in halves])

    def recurrence_steps(hd):
        return [functools.partial(recur, hd, c) for c in range(RET_STEP // RET_CHUNK)]

    def recur(hd, c):
        qk_cols = slice(hd * RET_DK, (hd + 1) * RET_DK)
        v_cols = slice(hd * RET_DV, (hd + 1) * RET_DV)
        rows = slice(c * RET_CHUNK, (c + 1) * RET_CHUNK)
        q = q_ref[rows, qk_cols]
        v = v_ref[rows, v_cols]
        scores = lax.dot_general(q, k_ref[rows, qk_cols], contract_last,
                                 preferred_element_type=F32) * decay_ref[hd]
        state = state_ref[hd]
        o = _dot(scores.astype(BF16), v) + _dot(q, state.astype(BF16)) * xi_ref[hd]
        state_ref[hd] = state * chunk_decay[hd] + lax.dot_general(
            kz_ref[rows, qk_cols], v, contract_first, preferred_element_type=F32)
        mu = jnp.mean(o, axis=-1, keepdims=True)
        d = o - mu
        var = jnp.mean(d * d, axis=-1, keepdims=True)
        normed = d * lax.rsqrt(var + GN_EPS) * gn_gain_ref[:, v_cols]
        o_ref[rows, v_cols] = (_silu(gate_ref[rows, v_cols]) * normed).astype(BF16)

    for step in projection_steps(0):
        step()
    for hd in range(RET_HEADS):
        ahead = projection_steps(hd + 1) if hd + 1 < RET_HEADS else []
        chunks = recurrence_steps(hd)
        for i in range(max(len(ahead), len(chunks))):
            for steps in (ahead, chunks):
                if i < len(steps):
                    steps[i]()


def _retention(xn, w_in, cos, sin, zeta_tab, decay, xi, gn_gain, chunk_decay, batch, seq):
    n = xn.shape[0]
    steps = seq // RET_STEP
    tok = lambda width: pl.BlockSpec((RET_STEP, width), lambda b, t: (b * steps + t, 0))
    staged = lambda width, dtype: pltpu.VMEM((RET_STEP, width), dtype)
    return pl.pallas_call(
        functools.partial(_retention_kernel, chunk_decay),
        grid=(batch, steps),
        in_specs=[
            tok(D_MODEL),
            _resident(w_in.shape, lambda b, t: (0, 0)),
            tok(ROPE_HALF), tok(ROPE_HALF),
            _resident((RET_STEP, RET_QK), lambda b, t: (0, 0)),
            _resident((RET_HEADS, RET_CHUNK, RET_CHUNK), lambda b, t: (0, 0, 0)),
            _resident((RET_HEADS, RET_CHUNK, RET_DV), lambda b, t: (0, 0, 0)),
            pl.BlockSpec((1, RET_V), lambda b, t: (0, 0)),
        ],
        out_specs=tok(RET_V),
        out_shape=jax.ShapeDtypeStruct((n, RET_V), BF16),
        scratch_shapes=[pltpu.VMEM((RET_HEADS, RET_DK, RET_DV), F32),
                        staged(RET_QK, BF16), staged(RET_QK, BF16),
                        staged(RET_QK, BF16), staged(RET_V, BF16), staged(RET_V, F32)],
        compiler_params=_params(("parallel", "arbitrary")),
        name="retention",
    )(xn, w_in, cos, sin, zeta_tab, decay, xi, gn_gain)


def _retention_constants():
    heads = jnp.arange(RET_HEADS, dtype=F32)
    log_gamma = jnp.log1p(-jnp.exp2(-5.0 - heads))
    idx = jnp.arange(RET_CHUNK, dtype=F32)
    rel = idx[:, None] - idx[None, :]
    inner = jnp.where(rel[None] >= 0,
                      jnp.exp(jnp.maximum(rel, 0.0)[None] * log_gamma[:, None, None]), 0.0)
    xi = jnp.exp((idx + 1.0)[None, :] * log_gamma[:, None])
    zeta = jnp.exp((RET_CHUNK - 1.0 - idx)[None, :] * log_gamma[:, None])
    return inner, xi, zeta


def _chunk_decay():
    return tuple(float((1.0 - 2.0 ** (-5 - hd)) ** RET_CHUNK) for hd in range(RET_HEADS))


def _stick_breaking_kernel(xn_ref, wq_ref, wk_ref, wv_ref, o_ref,
                           q_ref, k_ref, v_ref, qnew_ref, knew_ref, vnew_ref,
                           acc_ref, rest_ref):
    blk = SB_BLOCK
    seq = q_ref.shape[0]

    projections = ((wq_ref, q_ref, qnew_ref), (wk_ref, k_ref, knew_ref),
                   (wv_ref, v_ref, vnew_ref))

    ahead_rows = qnew_ref.shape[0]

    def project(r0):
        x = xn_ref[pl.ds(r0, ahead_rows), :]
        for w_ref, _, new_ref in projections:
            new_ref[...] = _dot(x, w_ref[...]).astype(BF16)

    def commit(r0):
        for _, dst, new_ref in projections:
            dst[pl.ds(r0, ahead_rows), :] = new_ref[...]

    scale = SB_DH ** -0.5
    row = lax.broadcasted_iota(jnp.int32, (blk, blk), 0)
    col = lax.broadcasted_iota(jnp.int32, (blk, blk), 1)
    causal = col < row
    r2 = lax.broadcasted_iota(jnp.int32, (2 * blk, 2 * blk), 0) & (blk - 1)
    c2 = lax.broadcasted_iota(jnp.int32, (2 * blk, 2 * blk), 1)
    neg_suffix_ones = jnp.where((r2 >= c2) | (c2 >= blk), -1.0, 0.0).astype(BF16)
    contract_last = (((1,), (1,)), ((), ()))
    heads = range(SB_GROUP)
    cols = [slice(g * SB_DH, (g + 1) * SB_DH) for g in heads]

    def tiles(qblocks, nblk, diagonal, side_work=None):
        width = nblk * blk
        sub = [slice(c * blk, (c + 1) * blk) for c in range(nblk)]
        chains = [(slot, g, q0, k0) for slot, q0, k0 in qblocks for g in heads]
        count = len(chains)
        raws, zs, sums, worst = [None] * count, [None] * count, [None] * count, {}

        def scores(i):
            _, g, q0, k0 = chains[i]
            raws[i] = lax.dot_general(q_ref[pl.ds(q0, blk), cols[g]],
                                      k_ref[pl.ds(k0, width), cols[g]],
                                      contract_last, preferred_element_type=F32)

        def suffix_sums(i):
            zs[i] = raws[i] * scale
            decayed = jnp.exp2(jnp.abs(raws[i]) * (-scale * LOG2_E))
            softplus = jnp.maximum(zs[i], 0.0) + jnp.log(1.0 + decayed)
            pieces = [softplus[:, s] for s in sub]
            if diagonal:
                pieces[-1] = jnp.where(causal, pieces[-1], 0.0)
            split = []
            for piece in pieces:
                hi = piece.astype(BF16)
                split.append(jnp.concatenate(
                    [hi, (piece - hi.astype(F32)).astype(BF16)], axis=1))
            both = _dot(jnp.concatenate(split, axis=0), neg_suffix_ones)
            sums[i] = [both[s] for s in sub]

        def weigh(i):
            slot, g, _, k0 = chains[i]
            rest = None if diagonal else rest_ref[slot, g]
            weights = [None] * nblk
            for c in reversed(range(nblk)):
                logw = zs[i][:, sub[c]] + sums[i][c][:, :blk]
                if rest is not None:
                    logw = logw + rest
                a = jnp.exp(logw)
                if diagonal and c == nblk - 1:
                    a = jnp.where(causal, a, 0.0)
                weights[c] = a.astype(BF16)
                row_sum = sums[i][c][:, blk:]
                rest = row_sum if rest is None else rest + row_sum
            part = _dot(jnp.concatenate(weights, axis=1), v_ref[pl.ds(k0, width), cols[g]])
            if diagonal:
                acc_ref[slot, g] = part
            else:
                acc_ref[slot, g] += part
            rest_ref[slot, g] = rest
            worst[slot] = rest if slot not in worst else jnp.maximum(worst[slot], rest)

        for i in range(count):
            scores(i)
        if side_work is not None:
            side_work()
        for stage in (suffix_sums, weigh):
            for i in range(count):
                stage(i)
        return [jnp.max(worst[slot]) for slot, _, _ in qblocks]

    def finish(slot, q0, j, worst):
        def more(c):
            j, worst = c
            return jnp.logical_and(j >= 0, worst > SB_SKIP_LOG)

        def step(c):
            j, _ = c
            return j - 1, tiles([(slot, q0, pl.multiple_of(j * blk, blk))], 1, False)[0]

        lax.while_loop(more, step, (j, worst))
        for g in heads:
            o_ref[pl.ds(q0, blk), cols[g]] = acc_ref[slot, g].astype(BF16)

    nq = seq // blk
    reach = SB_FIRST_PASS - 1
    prefix = reach + (nq - reach) % SB_QBLOCKS
    pending = list(range(0, (prefix + SB_QBLOCKS) * blk, ahead_rows))
    for qi in range(prefix):
        while pending and pending[0] < (qi + 1) * blk:
            project(pending[0])
            commit(pending.pop(0))
        beside = pending.pop(0) if pending else None
        tiles([(0, qi * blk, 0)], qi + 1, True,
              side_work=None if beside is None else functools.partial(project, beside))
        if beside is not None:
            commit(beside)
        for g in heads:
            o_ref[qi * blk:(qi + 1) * blk, cols[g]] = acc_ref[0, g].astype(BF16)
    for r0 in pending:
        project(r0)
        commit(r0)

    def q_blocks(step, carry):
        first = prefix + step * SB_QBLOCKS
        qblocks = [(slot, pl.multiple_of((first + slot) * blk, blk),
                    pl.multiple_of((first + slot - reach) * blk, blk))
                   for slot in range(SB_QBLOCKS)]
        ahead = pl.multiple_of(
            jnp.minimum(first + SB_QBLOCKS, nq - SB_QBLOCKS) * blk, blk)
        worst = tiles(qblocks, SB_FIRST_PASS, True,
                      side_work=functools.partial(project, ahead))
        for slot, q0, _ in qblocks:
            finish(slot, q0, first + slot - SB_FIRST_PASS, worst[slot])
        commit(ahead)
        return carry

    lax.fori_loop(0, (nq - prefix) // SB_QBLOCKS, q_blocks, 0)


def _stick_breaking(xn, w_in, batch, seq):
    xn = xn.reshape(batch, seq, D_MODEL)
    groups = SB_HEADS // SB_GROUP
    width = SB_GROUP * SB_DH
    weight = lambda part: pl.BlockSpec((D_MODEL, width), lambda b, g: (0, part * groups + g))
    staged = pltpu.VMEM((seq, width), BF16)
    ahead = pltpu.VMEM((SB_QBLOCKS * SB_BLOCK, width), BF16)
    out = pl.pallas_call(
        _stick_breaking_kernel,
        grid=(batch, groups),
        in_specs=[pl.BlockSpec((None, seq, D_MODEL), lambda b, g: (b, 0, 0)),
                  weight(0), weight(1), weight(2)],
        out_specs=pl.BlockSpec((None, seq, width), lambda b, g: (b, 0, g)),
        out_shape=jax.ShapeDtypeStruct((batch, seq, SB_WIDTH), BF16),
        scratch_shapes=[staged, staged, staged, ahead, ahead, ahead,
                        pltpu.VMEM((SB_QBLOCKS, SB_GROUP, SB_BLOCK, SB_DH), F32),
                        pltpu.VMEM((SB_QBLOCKS, SB_GROUP, SB_BLOCK, SB_BLOCK), F32)],
        compiler_params=_params(("parallel", "arbitrary")),
        name="stick_breaking",
    )(xn, w_in, w_in, w_in)
    return out.reshape(batch * seq, SB_WIDTH)


def kernel(x, p, positions, norm_gains, ffn_w_gate, ffn_w_up, ffn_w_down, ret_w_in,
           ret_gn_gain, ret_w_out, sb_w_in, sb_w_out, ple_w_gate, ple_w_proj):
    batch, seq, width = x.shape
    n = batch * seq
    assert width == D_MODEL and p.shape == (DEPTH, batch, seq, PLE_DIM)
    assert n % FFN_TM == 0 and n % ROPE_TM == 0 and seq % RET_STEP == 0
    assert RET_STEP % RET_CHUNK == 0 and seq % (SB_BLOCK * SB_QBLOCKS) == 0
    h = x.reshape(n, D_MODEL)
    p = p.reshape(DEPTH, n, PLE_DIM)
    gains = norm_gains
    ffn_stacks = (ffn_w_gate, ffn_w_up, ffn_w_down)
    cos, sin, ffn_weights = _rope_tables(positions, [(w, (0, 0)) for w in ffn_stacks])
    inner, xi, zeta = _retention_constants()
    xi_tab = jnp.broadcast_to(xi[:, :, None], (RET_HEADS, RET_CHUNK, RET_DV))
    zeta_tab = jnp.tile(jnp.repeat(zeta.T, RET_DK, axis=1), (RET_STEP // RET_CHUNK, 1))
    chunk_decay = _chunk_decay()

    for layer in range(DEPTH):
        j = layer // N_MIXERS
        retention = layer % N_MIXERS == 0
        mixer_stacks = (ret_w_in, ret_w_out) if retention else (sb_w_in, sb_w_out)
        h, xn, cast = _ffn(h, gains, layer, ffn_weights,
                           [(w, (j,)) for w in mixer_stacks]
                           + [(w, (layer, 1)) for w in ffn_stacks]
                           + [(ple_w_gate, (layer,)), (ple_w_proj, (layer,))])
        w_in, w_out, ffn_weights, ple_weights = cast[0], cast[1], cast[2:5], cast[5:7]
        if retention:
            mixed = _retention(xn, w_in, cos, sin, zeta_tab, inner, xi_tab,
                               ret_gn_gain[j].reshape(1, RET_V), chunk_decay, batch, seq)
        else:
            mixed = _stick_breaking(xn, w_in, batch, seq)
        next_ffn = [(w, (layer + 1, 0)) for w in ffn_stacks] if layer + 1 < DEPTH else []
        h, ffn_weights = _tail(h, mixed, p, gains, layer, w_out, ffn_weights, ple_weights,
                               next_ffn)
    return h.reshape(batch, seq, D_MODEL)
```

```python
import functools

import jax
import jax.numpy as jnp
from jax import lax
from jax.experimental import pallas as pl
from jax.experimental.pallas import tpu as pltpu

F32 = jnp.float32
BF16 = jnp.bfloat16

D_MODEL = 1024
DEPTH = 4
N_MIXERS = 2
PLE_DIM = 256
D_FF = 2816
FFN_RES_WEIGHT = 0.5
RET_HEADS = 4
RET_DK = D_MODEL // RET_HEADS
RET_QK = RET_HEADS * RET_DK
RET_DV = 2 * RET_DK
RET_V = RET_HEADS * RET_DV
RET_CHUNK = 256
ROPE_BASE = 10000.0
ROPE_HALF = RET_DK // 2
GN_EPS = 1e-5
SB_HEADS = 8
SB_DH = D_MODEL // SB_HEADS
SB_WIDTH = SB_HEADS * SB_DH
SB_BLOCK = 128
N_NORMS = 8
RMS_EPS = 1e-6
LOG2_E = 1.4426950408889634

VMEM_LIMIT_BYTES = 56 * 1024 * 1024

MXU_WIDTH = 256
FFN_TM = 512
ROPE_TM = 1024
RET_STEP = 512
RET_EPILOGUE_ROWS = 64
FFN_TF = MXU_WIDTH
FFN_PART_WEIGHTS = (1, 1)
SB_GROUP = 4
SB_FIRST_PASS = 3
SB_QBLOCKS = 2

SB_SKIP_LOG = -105.0


def _params(sem):
    return pltpu.CompilerParams(dimension_semantics=sem,
                                vmem_limit_bytes=VMEM_LIMIT_BYTES)


def _resident(shape, index_map):
    return pl.BlockSpec(shape, index_map, pipeline_mode=pl.Buffered(1))


def _rms(x, g):
    ms = jnp.mean(x * x, axis=-1, keepdims=True)
    return x * lax.rsqrt(ms + RMS_EPS) * g


def _dot(a, b):
    return jnp.dot(a, b, preferred_element_type=F32)


def _silu(x):
    return x * jax.nn.sigmoid(x)


def _gain(gains_ref, k):
    return gains_ref[k:k + 1, :]


def _row_parts(rows):
    bounds = [0] + [rows * sum(FFN_PART_WEIGHTS[:i + 1]) // sum(FFN_PART_WEIGHTS)
                    for i in range(len(FFN_PART_WEIGHTS))]
    return [slice(lo, hi) for lo, hi in zip(bounds[:-1], bounds[1:])]


def _half_ffn(parts, read_h, add_step, g_pre, g_post, wg_ref, wu_ref, wd_ref,
              xn_ref, hid_ref, side_work):
    def hidden(part, c):
        cols = slice(c * FFN_TF, (c + 1) * FFN_TF)
        xn = xn_ref[part, :]
        hid_ref[part, cols] = (_silu(_dot(xn, wg_ref[:, cols]))
                               * _dot(xn, wu_ref[:, cols])).astype(BF16)

    for i, part in enumerate(parts):
        xn_ref[part, :] = _rms(read_h(part), g_pre).astype(BF16)
        for c in range(D_FF // FFN_TF):
            hidden(part, c)
        if i == 0:
            side_work()
    g_step = g_post * FFN_RES_WEIGHT
    fs = [_dot(hid_ref[part, :], wd_ref[...]) for part in parts]
    for part, f in zip(parts, fs):
        add_step(part, _rms(f, g_step))


class _CastJobs:
    def __init__(self, jobs, steps):
        self.operands, self.in_specs, self.out_specs, self.out_shapes, self.shapes = [], [], [], [], []
        for w, lead in jobs:
            rows, cols = w.shape[-2:]
            slab = rows // steps
            assert slab * steps == rows and len(lead) == w.ndim - 2
            self.operands.append(w.reshape(w.shape[:-2] + (steps, slab, cols)))
            self.in_specs.append(pl.BlockSpec(
                (None,) * (len(lead) + 1) + (slab, cols),
                functools.partial(lambda lead, i: lead + (i, 0, 0), tuple(lead))))
            self.out_specs.append(pl.BlockSpec((None, slab, cols), lambda i: (i, 0, 0)))
            self.out_shapes.append(jax.ShapeDtypeStruct((steps, slab, cols), BF16))
            self.shapes.append((rows, cols))

    def __len__(self):
        return len(self.operands)

    def matrices(self, outputs):
        return [o.reshape(shape) for o, shape in zip(outputs, self.shapes)]


def _cast_slabs(src_refs, dst_refs):
    for src, dst in zip(src_refs, dst_refs):
        dst[...] = src[...].astype(BF16)


def _whole(w):
    return _resident(w.shape, lambda i: (0, 0))


def _ffn_kernel(n_cast, h_ref, gains_ref, wg_ref, wu_ref, wd_ref, *refs):
    cast_src, (o_ref, normed_ref, *cast_dst), (xn_ref, hid_ref) = (
        refs[:n_cast], refs[n_cast:2 * n_cast + 2], refs[2 * n_cast + 2:])

    def add_step(part, step):
        h = h_ref[part, :] + step
        o_ref[part, :] = h
        normed_ref[part, :] = _rms(h, _gain(gains_ref, 2)).astype(BF16)

    _half_ffn(_row_parts(h_ref.shape[0]), lambda part: h_ref[part, :], add_step,
              _gain(gains_ref, 0), _gain(gains_ref, 1), wg_ref, wu_ref, wd_ref,
              xn_ref, hid_ref, functools.partial(_cast_slabs, cast_src, cast_dst))


def _ffn_scratch():
    return [pltpu.VMEM((FFN_TM, D_MODEL), BF16), pltpu.VMEM((FFN_TM, D_FF), BF16)]


def _ffn(h, gains, layer, ffn_weights, cast_jobs):
    n = h.shape[0]
    steps = n // FFN_TM
    casts = _CastJobs(cast_jobs, steps)
    tile = pl.BlockSpec((FFN_TM, D_MODEL), lambda i: (i, 0))
    outs = pl.pallas_call(
        functools.partial(_ffn_kernel, len(casts)),
        grid=(steps,),
        in_specs=[
            tile,
            pl.BlockSpec((None, N_NORMS, D_MODEL), lambda i: (layer, 0, 0)),
        ] + [_whole(w) for w in ffn_weights] + casts.in_specs,
        out_specs=[tile, tile] + casts.out_specs,
        out_shape=[jax.ShapeDtypeStruct(h.shape, F32),
                   jax.ShapeDtypeStruct(h.shape, BF16)] + casts.out_shapes,
        scratch_shapes=_ffn_scratch(),
        compiler_params=_params(("parallel",)),
        name="ffn",
    )(h, gains, *ffn_weights, *casts.operands)
    return outs[0], outs[1], casts.matrices(outs[2:])


def _tail_kernel(n_cast, h_ref, mixed_ref, p_ref, gains_ref, wout_ref, wg_ref, wu_ref,
                 wd_ref, wgate_ref, wproj_ref, *refs):
    cast_src, (o_ref, *cast_dst), (xn_ref, hid_ref, h_scratch) = (
        refs[:n_cast], refs[n_cast:2 * n_cast + 1], refs[2 * n_cast + 1:])
    gain = functools.partial(_gain, gains_ref)
    parts = _row_parts(h_ref.shape[0])
    mixer_out = {part.start: _dot(mixed_ref[part, :], wout_ref[...]) for part in parts}

    def after_mixer(part):
        h = h_ref[part, :] + _rms(mixer_out[part.start], gain(3))
        h_scratch[part, :] = h
        return h

    def add_step_and_embedding(part, step):
        h = h_scratch[part, :] + step
        gate = jax.nn.sigmoid(_dot(_rms(h, gain(6)).astype(BF16), wgate_ref[...]))
        e = _dot(p_ref[part, :].astype(BF16), wproj_ref[...])
        o_ref[part, :] = h + _rms(gate * e, gain(7))

    _half_ffn(parts, after_mixer, add_step_and_embedding, gain(4), gain(5),
              wg_ref, wu_ref, wd_ref, xn_ref, hid_ref,
              functools.partial(_cast_slabs, cast_src, cast_dst))


def _tail(h, mixed, p, gains, layer, w_out, ffn_weights, ple_weights, cast_jobs):
    n = h.shape[0]
    steps = n // FFN_TM
    casts = _CastJobs(cast_jobs, steps)
    weights = [w_out, *ffn_weights, *ple_weights]
    outs = pl.pallas_call(
        functools.partial(_tail_kernel, len(casts)),
        grid=(steps,),
        in_specs=[
            pl.BlockSpec((FFN_TM, D_MODEL), lambda i: (i, 0)),
            pl.BlockSpec((FFN_TM, mixed.shape[1]), lambda i: (i, 0)),
            pl.BlockSpec((None, FFN_TM, PLE_DIM), lambda i: (layer, i, 0)),
            pl.BlockSpec((None, N_NORMS, D_MODEL), lambda i: (layer, 0, 0)),
        ] + [_whole(w) for w in weights] + casts.in_specs,
        out_specs=[pl.BlockSpec((FFN_TM, D_MODEL), lambda i: (i, 0))] + casts.out_specs,
        out_shape=[jax.ShapeDtypeStruct(h.shape, F32)] + casts.out_shapes,
        scratch_shapes=_ffn_scratch() + [pltpu.VMEM((FFN_TM, D_MODEL), F32)],
        compiler_params=_params(("parallel",)),
        name="tail",
    )(h, mixed, p, gains, *weights, *casts.operands)
    return outs[0], casts.matrices(outs[1:])


def _rope_table_kernel(n_cast, pos_ref, inv_ref, *refs):
    cast_src, (cos_ref, sin_ref, *cast_dst) = refs[:n_cast], refs[n_cast:]
    ang = pos_ref[...].astype(F32) * inv_ref[...]
    cos_ref[...] = jnp.cos(ang)
    sin_ref[...] = jnp.sin(ang)
    _cast_slabs(cast_src, cast_dst)


def _rope_tables(positions, cast_jobs):
    n = positions.size
    tm = ROPE_TM
    casts = _CastJobs(cast_jobs, n // tm)
    inv = ROPE_BASE ** (-jnp.arange(ROPE_HALF, dtype=F32) / ROPE_HALF)
    out = jax.ShapeDtypeStruct((n, ROPE_HALF), F32)
    outs = pl.pallas_call(
        functools.partial(_rope_table_kernel, len(casts)),
        grid=(n // tm,),
        in_specs=[pl.BlockSpec((tm, 1), lambda i: (i, 0)),
                  pl.BlockSpec((1, ROPE_HALF), lambda i: (0, 0))] + casts.in_specs,
        out_specs=[pl.BlockSpec((tm, ROPE_HALF), lambda i: (i, 0))] * 2 + casts.out_specs,
        out_shape=[out, out] + casts.out_shapes,
        compiler_params=_params(("parallel",)),
        name="rope_tables",
    )(positions.reshape(n, 1), inv.reshape(1, ROPE_HALF), *casts.operands)
    return outs[0], outs[1], casts.matrices(outs[2:])


def _retention_kernel(chunk_decay, xn_ref, w_ref, cos_ref, sin_ref, zeta_ref,
                      decay_ref, xi_ref, gn_gain_ref, o_ref,
                      state_ref, q_ref, k_ref, kz_ref, v_ref, gate_ref):
    @pl.when(pl.program_id(1) == 0)
    def _():
        state_ref[...] = jnp.zeros_like(state_ref)

    k_scale = RET_DK ** -0.5
    contract_last = (((1,), (1,)), ((), ()))
    contract_first = (((0,), (0,)), ((), ()))

    def projection_steps(hd):
        lo = hd * RET_DK
        mid = lo + ROPE_HALF
        hi = lo + RET_DK

        def rotated(col):
            t = _dot(xn_ref[...], w_ref[:, col:col + RET_DK])
            t1, t2 = t[:, :ROPE_HALF], t[:, ROPE_HALF:]
            cos = cos_ref[...]
            sin = sin_ref[...]
            return t1 * cos - t2 * sin, t1 * sin + t2 * cos

        def queries():
            q1, q2 = rotated(lo)
            q_ref[:, lo:mid] = q1.astype(BF16)
            q_ref[:, mid:hi] = q2.astype(BF16)

        def keys():
            k1, k2 = rotated(RET_QK + lo)
            k1 = k1 * k_scale
            k2 = k2 * k_scale
            k_ref[:, lo:mid] = k1.astype(BF16)
            k_ref[:, mid:hi] = k2.astype(BF16)
            kz_ref[:, lo:mid] = (k1 * zeta_ref[:, lo:mid]).astype(BF16)
            kz_ref[:, mid:hi] = (k2 * zeta_ref[:, mid:hi]).astype(BF16)

        def values(c):
            cols = slice(hd * RET_DV + c * MXU_WIDTH, hd * RET_DV + (c + 1) * MXU_WIDTH)
            v0 = 2 * RET_QK + cols.start
            v_ref[:, cols] = _dot(xn_ref[...], w_ref[:, v0:v0 + MXU_WIDTH]).astype(BF16)

        def gates(c):
            cols = slice(hd * RET_DV + c * MXU_WIDTH, hd * RET_DV + (c + 1) * MXU_WIDTH)
            g0 = 2 * RET_QK + RET_V + cols.start
            gate_ref[:, cols] = _dot(xn_ref[...], w_ref[:, g0:g0 + MXU_WIDTH])

        halves = range(RET_DV // MXU_WIDTH)
        return ([queries, keys] + [functools.partial(values, c) for c in halves]
                + [functools.partial(gates, c) for c in halves])

    def recurrence_steps(hd):
        return [functools.partial(recur, hd, c) for c in range(RET_STEP // RET_CHUNK)]

    def recur(hd, c):
        qk_cols = slice(hd * RET_DK, (hd + 1) * RET_DK)
        v_cols = slice(hd * RET_DV, (hd + 1) * RET_DV)
        rows = slice(c * RET_CHUNK, (c + 1) * RET_CHUNK)
        q = q_ref[rows, qk_cols]
        scores = (lax.dot_general(q, k_ref[rows, qk_cols], contract_last,
                                  preferred_element_type=F32) * decay_ref[hd]).astype(BF16)
        outs = []
        for c0 in range(0, RET_DV, MXU_WIDTH):
            half = slice(c0, c0 + MXU_WIDTH)
            v = v_ref[rows, hd * RET_DV + c0:hd * RET_DV + c0 + MXU_WIDTH]
            state = state_ref[hd, :, half]
            outs.append(_dot(scores, v) + _dot(q, state.astype(BF16)) * xi_ref[hd, :, half])
            state_ref[hd, :, half] = state * chunk_decay[hd] + lax.dot_general(
                kz_ref[rows, qk_cols], v, contract_first, preferred_element_type=F32)
        o = jnp.concatenate(outs, axis=1)
        for r0 in range(0, RET_CHUNK, RET_EPILOGUE_ROWS):
            sub = slice(r0, r0 + RET_EPILOGUE_ROWS)
            out_rows = slice(c * RET_CHUNK + r0, c * RET_CHUNK + r0 + RET_EPILOGUE_ROWS)
            o_sub = o[sub, :]
            mu = jnp.mean(o_sub, axis=-1, keepdims=True)
            d = o_sub - mu
            var = jnp.mean(d * d, axis=-1, keepdims=True)
            normed = d * lax.rsqrt(var + GN_EPS) * gn_gain_ref[:, v_cols]
            o_ref[out_rows, v_cols] = (_silu(gate_ref[out_rows, v_cols]) * normed).astype(BF16)

    for step in projection_steps(0):
        step()
    for hd in range(RET_HEADS):
        ahead = projection_steps(hd + 1) if hd + 1 < RET_HEADS else []
        chunks = recurrence_steps(hd)
        for i in range(max(len(ahead), len(chunks))):
            for steps in (ahead, chunks):
                if i < len(steps):
                    steps[i]()


def _retention(xn, w_in, cos, sin, zeta_tab, decay, xi, gn_gain, chunk_decay, batch, seq):
    n = xn.shape[0]
    steps = seq // RET_STEP
    tok = lambda width: pl.BlockSpec((RET_STEP, width), lambda b, t: (b * steps + t, 0))
    staged = lambda width, dtype: pltpu.VMEM((RET_STEP, width), dtype)
    return pl.pallas_call(
        functools.partial(_retention_kernel, chunk_decay),
        grid=(batch, steps),
        in_specs=[
            tok(D_MODEL),
            _resident(w_in.shape, lambda b, t: (0, 0)),
            tok(ROPE_HALF), tok(ROPE_HALF),
            _resident((RET_STEP, RET_QK), lambda b, t: (0, 0)),
            _resident((RET_HEADS, RET_CHUNK, RET_CHUNK), lambda b, t: (0, 0, 0)),
            _resident((RET_HEADS, RET_CHUNK, RET_DV), lambda b, t: (0, 0, 0)),
            pl.BlockSpec((1, RET_V), lambda b, t: (0, 0)),
        ],
        out_specs=tok(RET_V),
        out_shape=jax.ShapeDtypeStruct((n, RET_V), BF16),
        scratch_shapes=[pltpu.VMEM((RET_HEADS, RET_DK, RET_DV), F32),
                        staged(RET_QK, BF16), staged(RET_QK, BF16),
                        staged(RET_QK, BF16), staged(RET_V, BF16), staged(RET_V, F32)],
        compiler_params=_params(("parallel", "arbitrary")),
        name="retention",
    )(xn, w_in, cos, sin, zeta_tab, decay, xi, gn_gain)


def _retention_constants():
    heads = jnp.arange(RET_HEADS, dtype=F32)
    log_gamma = jnp.log1p(-jnp.exp2(-5.0 - heads))
    idx = jnp.arange(RET_CHUNK, dtype=F32)
    rel = idx[:, None] - idx[None, :]
    inner = jnp.where(rel[None] >= 0,
                      jnp.exp(jnp.maximum(rel, 0.0)[None] * log_gamma[:, None, None]), 0.0)
    xi = jnp.exp((idx + 1.0)[None, :] * log_gamma[:, None])
    zeta = jnp.exp((RET_CHUNK - 1.0 - idx)[None, :] * log_gamma[:, None])
    return inner, xi, zeta


def _chunk_decay():
    return tuple(float((1.0 - 2.0 ** (-5 - hd)) ** RET_CHUNK) for hd in range(RET_HEADS))


def _stick_breaking_kernel(xn_ref, wq_ref, wk_ref, wv_ref, o_ref,
                           q_ref, k_ref, v_ref, qnew_ref, knew_ref, vnew_ref,
                           acc_ref, rest_ref):
    blk = SB_BLOCK
    seq = q_ref.shape[0]

    projections = ((wq_ref, q_ref, qnew_ref), (wk_ref, k_ref, knew_ref),
                   (wv_ref, v_ref, vnew_ref))

    ahead_rows = qnew_ref.shape[0]

    def project(r0):
        x = xn_ref[pl.ds(r0, ahead_rows), :]
        for w_ref, _, new_ref in projections:
            new_ref[...] = _dot(x, w_ref[...]).astype(BF16)

    def commit(r0):
        for _, dst, new_ref in projections:
            dst[pl.ds(r0, ahead_rows), :] = new_ref[...]

    scale = SB_DH ** -0.5
    row = lax.broadcasted_iota(jnp.int32, (blk, blk), 0)
    col = lax.broadcasted_iota(jnp.int32, (blk, blk), 1)
    causal = col < row
    r2 = lax.broadcasted_iota(jnp.int32, (2 * blk, 2 * blk), 0) & (blk - 1)
    c2 = lax.broadcasted_iota(jnp.int32, (2 * blk, 2 * blk), 1)
    neg_suffix_ones = jnp.where((r2 >= c2) | (c2 >= blk), -1.0, 0.0).astype(BF16)
    contract_last = (((1,), (1,)), ((), ()))
    heads = range(SB_GROUP)
    cols = [slice(g * SB_DH, (g + 1) * SB_DH) for g in heads]

    def tiles(qblocks, nblk, diagonal, side_work=None):
        width = nblk * blk
        sub = [slice(c * blk, (c + 1) * blk) for c in range(nblk)]
        chains = [(slot, g, q0, k0) for slot, q0, k0 in qblocks for g in heads]
        count = len(chains)
        raws, zs, sums, worst = [None] * count, [None] * count, [None] * count, {}

        def scores(i):
            _, g, q0, k0 = chains[i]
            raws[i] = lax.dot_general(q_ref[pl.ds(q0, blk), cols[g]],
                                      k_ref[pl.ds(k0, width), cols[g]],
                                      contract_last, preferred_element_type=F32)

        def suffix_sums(i):
            zs[i] = raws[i] * scale
            decayed = jnp.exp2(jnp.abs(raws[i]) * (-scale * LOG2_E))
            softplus = jnp.maximum(zs[i], 0.0) + jnp.log(1.0 + decayed)
            pieces = [softplus[:, s] for s in sub]
            if diagonal:
                pieces[-1] = jnp.where(causal, pieces[-1], 0.0)
            split = []
            for piece in pieces:
                hi = piece.astype(BF16)
                split.append(jnp.concatenate(
                    [hi, (piece - hi.astype(F32)).astype(BF16)], axis=1))
            both = _dot(jnp.concatenate(split, axis=0), neg_suffix_ones)
            sums[i] = [both[s] for s in sub]

        def weigh(i):
            slot, g, _, k0 = chains[i]
            rest = None if diagonal else rest_ref[slot, g]
            weights = [None] * nblk
            for c in reversed(range(nblk)):
                logw = zs[i][:, sub[c]] + sums[i][c][:, :blk]
                if rest is not None:
                    logw = logw + rest
                a = jnp.exp(logw)
                if diagonal and c == nblk - 1:
                    a = jnp.where(causal, a, 0.0)
                weights[c] = a.astype(BF16)
                row_sum = sums[i][c][:, blk:]
                rest = row_sum if rest is None else rest + row_sum
            part = _dot(jnp.concatenate(weights, axis=1), v_ref[pl.ds(k0, width), cols[g]])
            if diagonal:
                acc_ref[slot, g] = part
            else:
                acc_ref[slot, g] += part
            rest_ref[slot, g] = rest
            worst[slot] = rest if slot not in worst else jnp.maximum(worst[slot], rest)

        for i in range(count):
            scores(i)
        if side_work is not None:
            side_work()
        for stage in (suffix_sums, weigh):
            for i in range(count):
                stage(i)
        return [jnp.max(worst[slot]) for slot, _, _ in qblocks]

    def finish(slot, q0, j, worst):
        def more(c):
            j, worst = c
            return jnp.logical_and(j >= 0, worst > SB_SKIP_LOG)

        def step(c):
            j, _ = c
            return j - 1, tiles([(slot, q0, pl.multiple_of(j * blk, blk))], 1, False)[0]

        lax.while_loop(more, step, (j, worst))
        for g in heads:
            o_ref[pl.ds(q0, blk), cols[g]] = acc_ref[slot, g].astype(BF16)

    nq = seq // blk
    reach = SB_FIRST_PASS - 1
    prefix = reach + (nq - reach) % SB_QBLOCKS
    pending = list(range(0, (prefix + SB_QBLOCKS) * blk, ahead_rows))
    for qi in range(prefix):
        while pending and pending[0] < (qi + 1) * blk:
            project(pending[0])
            commit(pending.pop(0))
        beside = pending.pop(0) if pending else None
        tiles([(0, qi * blk, 0)], qi + 1, True,
              side_work=None if beside is None else functools.partial(project, beside))
        if beside is not None:
            commit(beside)
        for g in heads:
            o_ref[qi * blk:(qi + 1) * blk, cols[g]] = acc_ref[0, g].astype(BF16)
    for r0 in pending:
        project(r0)
        commit(r0)

    def q_blocks(step, carry):
        first = prefix + step * SB_QBLOCKS
        qblocks = [(slot, pl.multiple_of((first + slot) * blk, blk),
                    pl.multiple_of((first + slot - reach) * blk, blk))
                   for slot in range(SB_QBLOCKS)]
        ahead = pl.multiple_of(
            jnp.minimum(first + SB_QBLOCKS, nq - SB_QBLOCKS) * blk, blk)
        worst = tiles(qblocks, SB_FIRST_PASS, True,
                      side_work=functools.partial(project, ahead))
        for slot, q0, _ in qblocks:
            finish(slot, q0, first + slot - SB_FIRST_PASS, worst[slot])
        commit(ahead)
        return carry

    lax.fori_loop(0, (nq - prefix) // SB_QBLOCKS, q_blocks, 0)


def _stick_breaking(xn, w_in, batch, seq):
    xn = xn.reshape(batch, seq, D_MODEL)
    groups = SB_HEADS // SB_GROUP
    width = SB_GROUP * SB_DH
    weight = lambda part: pl.BlockSpec((D_MODEL, width), lambda b, g: (0, part * groups + g))
    staged = pltpu.VMEM((seq, width), BF16)
    ahead = pltpu.VMEM((SB_QBLOCKS * SB_BLOCK, width), BF16)
    out = pl.pallas_call(
        _stick_breaking_kernel,
        grid=(batch, groups),
        in_specs=[pl.BlockSpec((None, seq, D_MODEL), lambda b, g: (b, 0, 0)),
                  weight(0), weight(1), weight(2)],
        out_specs=pl.BlockSpec((None, seq, width), lambda b, g: (b, 0, g)),
        out_shape=jax.ShapeDtypeStruct((batch, seq, SB_WIDTH), BF16),
        scratch_shapes=[staged, staged, staged, ahead, ahead, ahead,
                        pltpu.VMEM((SB_QBLOCKS, SB_GROUP, SB_BLOCK, SB_DH), F32),
                        pltpu.VMEM((SB_QBLOCKS, SB_GROUP, SB_BLOCK, SB_BLOCK), F32)],
        compiler_params=_params(("parallel", "arbitrary")),
        name="stick_breaking",
    )(xn, w_in, w_in, w_in)
    return out.reshape(batch * seq, SB_WIDTH)


def kernel(x, p, positions, norm_gains, ffn_w_gate, ffn_w_up, ffn_w_down, ret_w_in,
           ret_gn_gain, ret_w_out, sb_w_in, sb_w_out, ple_w_gate, ple_w_proj):
    batch, seq, width = x.shape
    n = batch * seq
    assert width == D_MODEL and p.shape == (DEPTH, batch, seq, PLE_DIM)
    assert n % FFN_TM == 0 and n % ROPE_TM == 0 and seq % RET_STEP == 0
    assert RET_STEP % RET_CHUNK == 0 and seq % (SB_BLOCK * SB_QBLOCKS) == 0
    h = x.reshape(n, D_MODEL)
    p = p.reshape(DEPTH, n, PLE_DIM)
    gains = norm_gains
    ffn_stacks = (ffn_w_gate, ffn_w_up, ffn_w_down)
    cos, sin, ffn_weights = _rope_tables(positions, [(w, (0, 0)) for w in ffn_stacks])
    inner, xi, zeta = _retention_constants()
    xi_tab = jnp.broadcast_to(xi[:, :, None], (RET_HEADS, RET_CHUNK, RET_DV))
    zeta_tab = jnp.tile(jnp.repeat(zeta.T, RET_DK, axis=1), (RET_STEP // RET_CHUNK, 1))
    chunk_decay = _chunk_decay()

    for layer in range(DEPTH):
        j = layer // N_MIXERS
        retention = layer % N_MIXERS == 0
        mixer_stacks = (ret_w_in, ret_w_out) if retention else (sb_w_in, sb_w_out)
        h, xn, cast = _ffn(h, gains, layer, ffn_weights,
                           [(w, (j,)) for w in mixer_stacks]
                           + [(w, (layer, 1)) for w in ffn_stacks]
                           + [(ple_w_gate, (layer,)), (ple_w_proj, (layer,))])
        w_in, w_out, ffn_weights, ple_weights = cast[0], cast[1], cast[2:5], cast[5:7]
        if retention:
            mixed = _retention(xn, w_in, cos, sin, zeta_tab, inner, xi_tab,
                               ret_gn_gain[j].reshape(1, RET_V), chunk_decay, batch, seq)
        else:
            mixed = _stick_breaking(xn, w_in, batch, seq)
        next_ffn = [(w, (layer + 1, 0)) for w in ffn_stacks] if layer + 1 < DEPTH else []
        h, ffn_weights = _tail(h, mixed, p, gains, layer, w_out, ffn_weights, ple_weights,
                               next_ffn)
    return h.reshape(batch, seq, D_MODEL)
```
